```python
import math
import jax, jax.numpy as jnp
from jax import lax
import numpy as np

D_MODEL = 2048
BATCH = 4
SEQ = 2048
DEPTH = 4

HEAD_DIM = 128
N_HEADS_MOBA = D_MODEL // (2 * HEAD_DIM)
N_HEADS_SB = D_MODEL // (2 * HEAD_DIM)
D_MOBA = N_HEADS_MOBA * HEAD_DIM
D_SB = N_HEADS_SB * HEAD_DIM
D_MIX = D_MOBA + D_SB
D_IN_PROJ = 3 * D_MOBA + 3 * D_SB
MOBA_BLOCK = 256
MOBA_TOPK = 3
MOBA_Q_CHUNK = 32
SB_Q_BLOCK = 128
REL_BUCKETS = 32
REL_MAX_DIST = 128
D_FF = 5632
N_SUBLAYERS = 3
N_MOD = 3
EPS = 1e-6

kernel_name = "hybrid_moba_stickbreaking_macaron_block"


def rmsnorm(x, gain):
    xf = x.astype(jnp.float32)
    y = xf * lax.rsqrt(jnp.mean(xf * xf, axis=-1, keepdims=True) + EPS)
    return (y * gain.astype(jnp.float32)).astype(x.dtype)


def rms_unit(x):
    xf = x.astype(jnp.float32)
    return (xf * lax.rsqrt(jnp.mean(xf * xf, axis=-1, keepdims=True) + EPS)).astype(x.dtype)


def swiglu(h, w_gate, w_up, w_down):
    return (jax.nn.silu(h @ w_gate) * (h @ w_up)) @ w_down


def t5_bucket(dist):
    n = jnp.maximum(dist, 0)
    max_exact = REL_BUCKETS // 2
    nf = jnp.maximum(n, 1).astype(jnp.float32)
    large = max_exact + (jnp.log(nf / max_exact) / math.log(REL_MAX_DIST / max_exact)
                         * (REL_BUCKETS - max_exact)).astype(jnp.int32)
    large = jnp.minimum(large, REL_BUCKETS - 1)
    return jnp.where(n < max_exact, n, large)


def moba_attention(q, k, v, rel_bias):
    B, H, S, Dh = q.shape
    nb = -(-S // MOBA_BLOCK)
    s_pad = nb * MOBA_BLOCK
    pad = ((0, 0), (0, 0), (0, s_pad - S), (0, 0))
    q, k, v = jnp.pad(q, pad), jnp.pad(k, pad), jnp.pad(v, pad)
    k_blk = k.reshape(B, H, nb, MOBA_BLOCK, Dh)
    v_blk = v.reshape(B, H, nb, MOBA_BLOCK, Dh)
    k_mean = jnp.mean(k_blk, axis=3)
    top_k = min(MOBA_TOPK, nb)
    scale = Dh ** -0.5
    bias_t = rel_bias.T.astype(jnp.float32)
    bi = jnp.arange(B)[:, None, None, None]
    hi = jnp.arange(H)[None, :, None, None]
    hi5 = jnp.arange(H)[None, :, None, None, None]
    blk_offsets = jnp.arange(MOBA_BLOCK)

    def chunk(ci):
        q0 = ci * MOBA_Q_CHUNK
        blk = q0 // MOBA_BLOCK
        qc = lax.dynamic_slice_in_dim(q, q0, MOBA_Q_CHUNK, axis=2)
        q_pos = q0 + jnp.arange(MOBA_Q_CHUNK)
        gate = jnp.einsum('bhqd,bhnd->bhqn', qc, k_mean).astype(jnp.float32)
        gate = jnp.where(jnp.arange(nb) < blk, gate, -jnp.inf)
        _, sel = lax.top_k(gate, top_k)
        sel_valid = sel < blk
        k_sel = k_blk[bi, hi, sel]
        v_sel = v_blk[bi, hi, sel]
        s_sel = jnp.einsum('bhqd,bhqnkd->bhqnk', qc, k_sel).astype(jnp.float32) * scale
        k_pos_sel = sel[..., None] * MOBA_BLOCK + blk_offsets
        s_sel = s_sel + bias_t[hi5, t5_bucket(q_pos[None, None, :, None, None] - k_pos_sel)]
        s_sel = jnp.where(sel_valid[..., None], s_sel, -jnp.inf)
        k_own = lax.dynamic_slice_in_dim(k, blk * MOBA_BLOCK, MOBA_BLOCK, axis=2)
        v_own = lax.dynamic_slice_in_dim(v, blk * MOBA_BLOCK, MOBA_BLOCK, axis=2)
        rel_own = q_pos[:, None] - (blk * MOBA_BLOCK + blk_offsets)[None, :]
        s_own = jnp.einsum('bhqd,bhkd->bhqk', qc, k_own).astype(jnp.float32) * scale
        s_own = s_own + bias_t[:, t5_bucket(rel_own)]
        s_own = jnp.where(rel_own >= 0, s_own, -jnp.inf)
        logits = jnp.concatenate([s_sel.reshape(B, H, MOBA_Q_CHUNK, top_k * MOBA_BLOCK), s_own], axis=-1)
        p = jax.nn.softmax(logits, axis=-1)
        p_sel = p[..., :top_k * MOBA_BLOCK].reshape(B, H, MOBA_Q_CHUNK, top_k, MOBA_BLOCK)
        p_own = p[..., top_k * MOBA_BLOCK:]
        out = (jnp.einsum('bhqnk,bhqnkd->bhqd', p_sel.astype(v.dtype), v_sel)
               + jnp.einsum('bhqk,bhkd->bhqd', p_own.astype(v.dtype), v_own))
        return out

    outs = lax.map(chunk, jnp.arange(s_pad // MOBA_Q_CHUNK))
    outs = jnp.transpose(outs, (1, 2, 0, 3, 4)).reshape(B, H, s_pad, Dh)
    return outs[:, :, :S]


def stick_breaking_attention(q, k, v):
    B, H, S, Dh = q.shape
    scale = Dh ** -0.5
    k_pos = jnp.arange(S)

    def block(i):
        q0 = i * SB_Q_BLOCK
        qb = lax.dynamic_slice_in_dim(q, q0, SB_Q_BLOCK, axis=2)
        q_pos = q0 + jnp.arange(SB_Q_BLOCK)
        z = jnp.einsum('bhqd,bhkd->bhqk', qb, k).astype(jnp.float32) * scale
        strict = k_pos[None, :] < q_pos[:, None]
        log_fail = jnp.where(strict, jax.nn.log_sigmoid(-z), 0.0)
        csum = jnp.cumsum(log_fail, axis=-1)
        remaining = csum[..., -1:] - csum
        a = jnp.where(strict, jnp.exp(jax.nn.log_sigmoid(z) + remaining), 0.0)
        return jnp.einsum('bhqk,bhkd->bhqd', a.astype(v.dtype), v)

    outs = lax.map(block, jnp.arange(S // SB_Q_BLOCK))
    return jnp.transpose(outs, (1, 2, 0, 3, 4)).reshape(B, H, S, Dh)


def hybrid_mixer(h, w_in, out_gain, w_out, rel_bias):
    B, S, _ = h.shape
    proj = h @ w_in
    q_a, k_a, v_a, q_b, k_b, v_b = jnp.split(
        proj, [D_MOBA, 2 * D_MOBA, 3 * D_MOBA, 3 * D_MOBA + D_SB, 3 * D_MOBA + 2 * D_SB], axis=-1)

    def heads(t, n):
        return t.reshape(B, S, n, HEAD_DIM).transpose(0, 2, 1, 3)

    o_a = moba_attention(heads(q_a, N_HEADS_MOBA), heads(k_a, N_HEADS_MOBA), heads(v_a, N_HEADS_MOBA), rel_bias)
    o_b = stick_breaking_attention(heads(q_b, N_HEADS_SB), heads(k_b, N_HEADS_SB), heads(v_b, N_HEADS_SB))
    o = rms_unit(jnp.concatenate([o_a, o_b], axis=1))
    o = o.transpose(0, 2, 1, 3).reshape(B, S, D_MIX) * out_gain
    return o @ w_out


def sandwich(x, y_fn, g_pre, g_post, shift, scale, gate, res_w):
    h = rmsnorm(x, g_pre) * (1 + scale[:, None, :]) + shift[:, None, :]
    y = rmsnorm(y_fn(h), g_post)
    return x + res_w * gate[:, None, :] * y


def setup_inputs(seed: int = 0) -> dict:
    key = jax.random.key(seed)
    ks = jax.random.split(key, 16)
    f32 = jnp.float32
    D, F = D_MODEL, D_FF

    def normal(k, shape, s):
        return jax.random.normal(k, shape, f32) * s

    return {
        "x": normal(ks[0], (BATCH, SEQ, D), 1.0),
        "c": normal(ks[1], (BATCH, D), 1.0),
        "ada_w": normal(ks[2], (DEPTH, D, N_SUBLAYERS * N_MOD * D), D ** -0.5),
        "ada_b": normal(ks[3], (DEPTH, N_SUBLAYERS * N_MOD * D), 0.02),
        "norm_pre": 1.0 + normal(ks[4], (DEPTH, N_SUBLAYERS, D), 0.05),
        "norm_post": 1.0 + normal(ks[5], (DEPTH, N_SUBLAYERS, D), 0.05),
        "ffn1_w_gate": normal(ks[6], (DEPTH, D, F), D ** -0.5),
        "ffn1_w_up": normal(ks[7], (DEPTH, D, F), D ** -0.5),
        "ffn1_w_down": normal(ks[8], (DEPTH, F, D), F ** -0.5),
        "mix_w_in": normal(ks[9], (DEPTH, D, D_IN_PROJ), D ** -0.5),
        "mix_out_gain": 1.0 + normal(ks[10], (DEPTH, D_MIX), 0.05),
        "mix_w_out": normal(ks[11], (DEPTH, D_MIX, D), D_MIX ** -0.5),
        "rel_bias": normal(ks[12], (REL_BUCKETS, N_HEADS_MOBA), 0.5),
        "ffn2_w_gate": normal(ks[13], (DEPTH, D, F), D ** -0.5),
        "ffn2_w_up": normal(ks[14], (DEPTH, D, F), D ** -0.5),
        "ffn2_w_down": normal(ks[15], (DEPTH, F, D), F ** -0.5),
    }


def reference(x, c, ada_w, ada_b, norm_pre, norm_post, ffn1_w_gate, ffn1_w_up, ffn1_w_down,
              mix_w_in, mix_out_gain, mix_w_out, rel_bias, ffn2_w_gate, ffn2_w_up, ffn2_w_down):
    B = x.shape[0]
    c_act = jax.nn.silu(c)
    for l in range(DEPTH):
        mod = (c_act @ ada_w[l] + ada_b[l]).reshape(B, N_SUBLAYERS, N_MOD, D_MODEL)
        x = sandwich(x, lambda h: swiglu(h, ffn1_w_gate[l], ffn1_w_up[l], ffn1_w_down[l]),
                     norm_pre[l, 0], norm_post[l, 0], mod[:, 0, 0], mod[:, 0, 1], mod[:, 0, 2], 0.5)
        x = sandwich(x, lambda h: hybrid_mixer(h, mix_w_in[l], mix_out_gain[l], mix_w_out[l], rel_bias),
                     norm_pre[l, 1], norm_post[l, 1], mod[:, 1, 0], mod[:, 1, 1], mod[:, 1, 2], 1.0)
        x = sandwich(x, lambda h: swiglu(h, ffn2_w_gate[l], ffn2_w_up[l], ffn2_w_down[l]),
                     norm_pre[l, 2], norm_post[l, 2], mod[:, 2, 0], mod[:, 2, 1], mod[:, 2, 2], 0.5)
    return x
```

```python
import functools
import math

import numpy as np
import jax
import jax.numpy as jnp
from jax import lax
from jax.experimental import pallas as pl
from jax.experimental.pallas import tpu as pltpu

F32 = jnp.float32
BF16 = jnp.bfloat16

HEAD_DIM = 128
MOBA_BLOCK = 256
MOBA_TOPK = 3
REL_BUCKETS = 32
REL_MAX_DIST = 128
N_SUB = 3
N_MOD = 3
EPS = 1e-6
SB_BLOCK = 256

VMEM_LIMIT = 56 * 1024 * 1024


def _cparams(sem):
    return pltpu.CompilerParams(dimension_semantics=sem, vmem_limit_bytes=VMEM_LIMIT)


def _dot(a, b):
    return jnp.dot(a, b, preferred_element_type=F32)


def _dot_nt(a, b):
    return lax.dot_general(a, b, (((1,), (1,)), ((), ())), preferred_element_type=F32)


def _rms(x):
    return x * lax.rsqrt(jnp.mean(x * x, axis=-1, keepdims=True) + EPS)


def _ada_kernel(c_ref, w_ref, b_ref, o_ref):
    c = c_ref[...]
    ca = (c * jax.nn.sigmoid(c)).astype(BF16)
    o_ref[...] = _dot(ca, w_ref[...].astype(BF16)) + b_ref[...]


def _ada_mods(c_pad, ada_w, ada_b):
    L, D, N = ada_w.shape
    tn = min(1024, D)
    return pl.pallas_call(
        _ada_kernel,
        grid=(L, N // tn),
        in_specs=[
            pl.BlockSpec((8, D), lambda l, n: (0, 0)),
            pl.BlockSpec((None, D, tn), lambda l, n: (l, 0, n)),
            pl.BlockSpec((None, 1, tn), lambda l, n: (l, 0, n)),
        ],
        out_specs=pl.BlockSpec((None, 8, tn), lambda l, n: (l, 0, n)),
        out_shape=jax.ShapeDtypeStruct((L, 8, N), F32),
        compiler_params=_cparams(("parallel", "parallel")),
    )(c_pad, ada_w, ada_b.reshape(L, 1, N))


def _ffn_kernel(x_ref, mod_ref, gpre_ref, gpost_ref, wg_ref, wu_ref, wd_ref, o_ref,
                h_ref, acc_ref, *, sub, res_w):
    j = pl.program_id(1)

    @pl.when(j == 0)
    def _():
        y = _rms(x_ref[...]) * gpre_ref[...]
        shift = mod_ref[3 * sub:3 * sub + 1, :]
        scale = mod_ref[3 * sub + 1:3 * sub + 2, :]
        h_ref[...] = (y * (1.0 + scale) + shift).astype(BF16)
        acc_ref[...] = jnp.zeros_like(acc_ref)

    h = h_ref[...]
    g = _dot(h, wg_ref[...])
    u = _dot(h, wu_ref[...])
    a = (g * jax.nn.sigmoid(g) * u).astype(BF16)
    acc_ref[...] += _dot(a, wd_ref[...])

    @pl.when(j == pl.num_programs(1) - 1)
    def _():
        gate = mod_ref[3 * sub + 2:3 * sub + 3, :]
        yn = _rms(acc_ref[...]) * gpost_ref[...]
        o_ref[...] = x_ref[...] + res_w * gate * yn


def _ffn_sublayer(x2, mods_l, gpre, gpost, wg, wu, wd, l, *, sub, seq, tm=512, tf=512):
    M, D = x2.shape
    F = wg.shape[-1]
    tpb = seq // tm
    return pl.pallas_call(
        functools.partial(_ffn_kernel, sub=sub, res_w=0.5),
        grid=(M // tm, F // tf),
        in_specs=[
            pl.BlockSpec((tm, D), lambda i, j: (i, 0)),
            pl.BlockSpec((None, N_SUB * N_MOD, D), lambda i, j: (i // tpb, 0, 0)),
            pl.BlockSpec((1, D), lambda i, j: (0, 0)),
            pl.BlockSpec((1, D), lambda i, j: (0, 0)),
            pl.BlockSpec((None, D, tf), lambda i, j: (l, 0, j)),
            pl.BlockSpec((None, D, tf), lambda i, j: (l, 0, j)),
            pl.BlockSpec((None, tf, D), lambda i, j: (l, j, 0)),
        ],
        out_specs=pl.BlockSpec((tm, D), lambda i, j: (i, 0)),
        out_shape=jax.ShapeDtypeStruct((M, D), F32),
        scratch_shapes=[pltpu.VMEM((tm, D), BF16), pltpu.VMEM((tm, D), F32)],
        compiler_params=_cparams(("parallel", "arbitrary")),
    )(x2, mods_l, gpre, gpost, wg, wu, wd)


def _inproj_kernel(x_ref, mod_ref, gpre_ref, w_ref, o_ref, h_ref, *, heads_per_tile):
    @pl.when(pl.program_id(1) == 0)
    def _():
        y = _rms(x_ref[...]) * gpre_ref[...]
        shift = mod_ref[3:4, :]
        scale = mod_ref[4:5, :]
        h_ref[...] = (y * (1.0 + scale) + shift).astype(BF16)

    res = _dot(h_ref[...], w_ref[...])
    for hh in range(heads_per_tile):
        o_ref[hh] = res[:, hh * HEAD_DIM:(hh + 1) * HEAD_DIM].astype(BF16)


def _inproj(x2, mods_l, gpre, w_in, l, *, batch, seq, tm=512, tn=512):
    M, D = x2.shape
    N = w_in.shape[-1]
    tpb = seq // tm
    hpt = tn // HEAD_DIM
    return pl.pallas_call(
        functools.partial(_inproj_kernel, heads_per_tile=hpt),
        grid=(M // tm, N // tn),
        in_specs=[
            pl.BlockSpec((tm, D), lambda i, n: (i, 0)),
            pl.BlockSpec((None, N_SUB * N_MOD, D), lambda i, n: (i // tpb, 0, 0)),
            pl.BlockSpec((1, D), lambda i, n: (0, 0)),
            pl.BlockSpec((None, D, tn), lambda i, n: (l, 0, n)),
        ],
        out_specs=pl.BlockSpec((None, hpt, tm, HEAD_DIM), lambda i, n: (i // tpb, n, i % tpb, 0)),
        out_shape=jax.ShapeDtypeStruct((batch, N // HEAD_DIM, seq, HEAD_DIM), BF16),
        scratch_shapes=[pltpu.VMEM((tm, D), BF16)],
        compiler_params=_cparams(("parallel", "arbitrary")),
    )(x2, mods_l, gpre, w_in)


def _t5_bucket_np(dist):
    n = np.maximum(dist, 0)
    max_exact = REL_BUCKETS // 2
    nf = np.maximum(n, 1).astype(np.float32)
    large = max_exact + (np.log(nf / np.float32(max_exact)) / np.float32(math.log(REL_MAX_DIST / max_exact))
                         * np.float32(REL_BUCKETS - max_exact)).astype(np.int32)
    large = np.minimum(large, REL_BUCKETS - 1)
    return np.where(n < max_exact, n, large).astype(np.int32)


def _bias_table_kernel(rb_ref, bucket_ref, o_ref):
    h = pl.program_id(0)
    for t in range(2):
        bucket = bucket_ref[t]
        tab = jnp.zeros(bucket.shape, F32)
        for b in range(REL_BUCKETS):
            tab = jnp.where(bucket == b, rb_ref[h, b], tab)
        if t == 0:
            row = lax.broadcasted_iota(jnp.int32, bucket.shape, 0)
            col = lax.broadcasted_iota(jnp.int32, bucket.shape, 1)
            tab = jnp.where(row >= col, tab, -jnp.inf)
        o_ref[t] = tab


def _bias_tables(rel_bias_t):
    H = rel_bias_t.shape[0]
    r = np.arange(MOBA_BLOCK)
    rel = r[:, None] - r[None, :]
    buckets = jnp.asarray(np.stack([_t5_bucket_np(rel), _t5_bucket_np(rel + MOBA_BLOCK)]))
    return pl.pallas_call(
        _bias_table_kernel,
        grid=(H,),
        in_specs=[
            pl.BlockSpec(memory_space=pltpu.SMEM),
            pl.BlockSpec((2, MOBA_BLOCK, MOBA_BLOCK), lambda h: (0, 0, 0)),
        ],
        out_specs=pl.BlockSpec((None, 2, MOBA_BLOCK, MOBA_BLOCK), lambda h: (h, 0, 0, 0)),
        out_shape=jax.ShapeDtypeStruct((H, 2, MOBA_BLOCK, MOBA_BLOCK), F32),
        compiler_params=_cparams(("arbitrary",)),
    )(rel_bias_t, buckets)


def _head_out(acc, gain):
    return (_rms(acc) * gain).astype(BF16)


def _moba_head(h, rb_ref, q_ref, k_ref, v_ref, tab_ref, gain_ref, o_ref, cm_ref, *, nb):
    BLK = MOBA_BLOCK
    b_far = rb_ref[h, REL_BUCKETS - 1]
    scale = HEAD_DIM ** -0.5

    kf = k_ref[...].astype(F32).reshape(nb, BLK, HEAD_DIM)
    k_mean = jnp.mean(kf, axis=1)
    k_mean = jnp.concatenate([k_mean, jnp.zeros((HEAD_DIM - nb, HEAD_DIM), F32)], axis=0).astype(BF16)
    col = lax.broadcasted_iota(jnp.int32, (BLK, HEAD_DIM), 1)

    def blk(ref, j):
        return ref[pl.ds(pl.multiple_of(j * BLK, BLK), BLK), :]

    for i in range(nb):
        qi = q_ref[i * BLK:(i + 1) * BLK, :]
        s = _dot_nt(qi, k_ref[i * BLK:(i + 1) * BLK, :]) * scale + tab_ref[0]
        m = jnp.max(s, axis=-1, keepdims=True)
        p = jnp.exp(s - m)
        l = jnp.sum(p, axis=-1, keepdims=True)
        acc = _dot(p.astype(BF16), v_ref[i * BLK:(i + 1) * BLK, :])

        if i > 0:
            gate = jnp.where(col < i, _dot_nt(qi, k_mean), -jnp.inf)
            rank = jnp.zeros((BLK, HEAD_DIM), jnp.int32)
            for jp in range(i):
                cj = gate[:, jp:jp + 1]
                beats = (cj > gate) | ((cj == gate) & (jp < col))
                rank = rank + beats.astype(jnp.int32)
            selw = jnp.where((rank < MOBA_TOPK) & (col < i), 0.0, -jnp.inf)

            def update(carry, s, vj):
                m, l, acc = carry
                m_new = jnp.maximum(m, jnp.max(s, axis=-1, keepdims=True))
                alpha = jnp.exp(m - m_new)
                p = jnp.exp(s - m_new)
                l = alpha * l + jnp.sum(p, axis=-1, keepdims=True)
                acc = alpha * acc + _dot(p.astype(BF16), vj)
                return m_new, l, acc

            s = (_dot_nt(qi, k_ref[(i - 1) * BLK:i * BLK, :]) * scale + tab_ref[1]
                 + selw[:, i - 1:i])
            m, l, acc = update((m, l, acc), s, v_ref[(i - 1) * BLK:i * BLK, :])

            if i > 1:
                for j in range(i - 1):
                    cm_ref[j] = jnp.broadcast_to(selw[:, j:j + 1] + b_far, (BLK, BLK))

                def far(j, carry):
                    s = _dot_nt(qi, blk(k_ref, j)) * scale + cm_ref[j]
                    return update(carry, s, blk(v_ref, j))

                m, l, acc = lax.fori_loop(0, i - 1, far, (m, l, acc))

        o_ref[i * BLK:(i + 1) * BLK, :] = _head_out(acc / l, gain_ref[...])


def _sb_head(q_ref, k_ref, v_ref, gain_ref, o_ref, *, nb):
    BLK = SB_BLOCK
    scale = HEAD_DIM ** -0.5
    row = lax.broadcasted_iota(jnp.int32, (BLK, BLK), 0)
    col = lax.broadcasted_iota(jnp.int32, (BLK, BLK), 1)
    strict = col < row
    after = jnp.where(row > col, 1.0, 0.0).astype(BF16)

    def tail_sums(lf):
        hi = lf.astype(BF16)
        lo = (lf - hi.astype(F32)).astype(BF16)
        return _dot(hi, after) + _dot(lo, after)

    def log_fail(z):
        return -(jnp.maximum(z, 0.0) + jnp.log1p(jnp.exp(-jnp.abs(z))))

    def blk(ref, j):
        return ref[pl.ds(pl.multiple_of(j * BLK, BLK), BLK), :]

    for i in range(nb):
        qi = q_ref[i * BLK:(i + 1) * BLK, :]
        z = _dot_nt(qi, k_ref[i * BLK:(i + 1) * BLK, :]) * scale
        lf_all = log_fail(z)
        lf = jnp.where(strict, lf_all, 0.0)
        w = tail_sums(lf)
        a = jnp.where(strict, jnp.exp(z + lf_all + w), 0.0)
        acc = _dot(a.astype(BF16), v_ref[i * BLK:(i + 1) * BLK, :])
        rem = w[:, 0:1] + lf[:, 0:1]

        if i > 0:
            def past(t, carry):
                rem, acc = carry
                j = i - 1 - t
                z = _dot_nt(qi, blk(k_ref, j)) * scale
                lf = log_fail(z)
                w = tail_sums(lf)
                a = jnp.exp(z + lf + w + rem)
                acc = acc + _dot(a.astype(BF16), blk(v_ref, j))
                return rem + w[:, 0:1] + lf[:, 0:1], acc

            rem, acc = lax.fori_loop(0, i, past, (rem, acc))

        o_ref[i * BLK:(i + 1) * BLK, :] = _head_out(acc, gain_ref[...])


def _attn_kernel(rb_ref, q_ref, k_ref, v_ref, tab_ref, gain_ref, o_ref, cm_ref, *, nb, n_heads_a):
    g = pl.program_id(1)

    @pl.when(g < n_heads_a)
    def _():
        _moba_head(g, rb_ref, q_ref, k_ref, v_ref, tab_ref, gain_ref, o_ref, cm_ref, nb=nb)

    @pl.when(g >= n_heads_a)
    def _():
        _sb_head(q_ref, k_ref, v_ref, gain_ref, o_ref, nb=nb)


def _attention(proj, rel_bias_t, tables, gain2, *, n_heads_a, n_heads_b):
    B, _, S, _ = proj.shape
    nb = S // MOBA_BLOCK
    blk = (None, None, S, HEAD_DIM)

    def col_head(which):
        off_a = which * n_heads_a
        off_b = 3 * n_heads_a + which * n_heads_b - n_heads_a
        return lambda b, g: (b, jnp.where(g < n_heads_a, off_a + g, off_b + g), 0, 0)

    return pl.pallas_call(
        functools.partial(_attn_kernel, nb=nb, n_heads_a=n_heads_a),
        grid=(B, n_heads_a + n_heads_b),
        in_specs=[
            pl.BlockSpec(memory_space=pltpu.SMEM),
            pl.BlockSpec(blk, col_head(0)),
            pl.BlockSpec(blk, col_head(1)),
            pl.BlockSpec(blk, col_head(2)),
            pl.BlockSpec((None, 2, MOBA_BLOCK, MOBA_BLOCK),
                         lambda b, g: (jnp.minimum(g, n_heads_a - 1), 0, 0, 0)),
            pl.BlockSpec((1, HEAD_DIM), lambda b, g: (0, g)),
        ],
        out_specs=pl.BlockSpec((None, S, HEAD_DIM), lambda b, g: (b, 0, g)),
        out_shape=jax.ShapeDtypeStruct((B, S, (n_heads_a + n_heads_b) * HEAD_DIM), BF16),
        scratch_shapes=[pltpu.VMEM((nb, MOBA_BLOCK, MOBA_BLOCK), F32)],
        compiler_params=_cparams(("parallel", "parallel")),
    )(rel_bias_t, proj, proj, proj, tables, gain2)


def _outproj_kernel(x_ref, o_ref, mod_ref, gpost_ref, w_ref, out_ref):
    y = _dot(o_ref[...], w_ref[...])
    gate = mod_ref[5:6, :]
    out_ref[...] = x_ref[...] + gate * (_rms(y) * gpost_ref[...])


def _outproj(x2, o2, mods_l, gpost, w_out, l, *, seq, tm=512):
    M, D = x2.shape
    K = o2.shape[-1]
    tpb = seq // tm
    return pl.pallas_call(
        _outproj_kernel,
        grid=(M // tm,),
        in_specs=[
            pl.BlockSpec((tm, D), lambda i: (i, 0)),
            pl.BlockSpec((tm, K), lambda i: (i, 0)),
            pl.BlockSpec((None, N_SUB * N_MOD, D), lambda i: (i // tpb, 0, 0)),
            pl.BlockSpec((1, D), lambda i: (0, 0)),
            pl.BlockSpec((None, K, D), lambda i: (l, 0, 0)),
        ],
        out_specs=pl.BlockSpec((tm, D), lambda i: (i, 0)),
        out_shape=jax.ShapeDtypeStruct((M, D), F32),
        compiler_params=_cparams(("parallel",)),
    )(x2, o2, mods_l, gpost, w_out)


def kernel(x, c, ada_w, ada_b, norm_pre, norm_post, ffn1_w_gate, ffn1_w_up, ffn1_w_down,
           mix_w_in, mix_out_gain, mix_w_out, rel_bias, ffn2_w_gate, ffn2_w_up, ffn2_w_down):
    B, S, D = x.shape
    L = ada_w.shape[0]
    n_heads_a = rel_bias.shape[1]
    d_mix = mix_w_out.shape[1]
    n_heads_b = d_mix // HEAD_DIM - n_heads_a
    assert S % MOBA_BLOCK == 0 and mix_w_in.shape[-1] == 3 * d_mix

    c_pad = jnp.pad(c, ((0, 8 - B), (0, 0)))
    mods = _ada_mods(c_pad, ada_w, ada_b)[:, :B].reshape(L, B, N_SUB * N_MOD, D)
    rel_bias_t = rel_bias.T
    tables = _bias_tables(rel_bias_t)

    w1g, w1u, w1d = (w.astype(BF16) for w in (ffn1_w_gate, ffn1_w_up, ffn1_w_down))
    w2g, w2u, w2d = (w.astype(BF16) for w in (ffn2_w_gate, ffn2_w_up, ffn2_w_down))
    w_in = mix_w_in.astype(BF16)
    w_out = mix_w_out.astype(BF16)

    x2 = x.reshape(B * S, D)
    for l in range(L):
        gpre = norm_pre[l].reshape(N_SUB, 1, D)
        gpost = norm_post[l].reshape(N_SUB, 1, D)
        gain2 = mix_out_gain[l].reshape(1, d_mix)
        x2 = _ffn_sublayer(x2, mods[l], gpre[0], gpost[0], w1g, w1u, w1d, l, sub=0, seq=S)
        proj = _inproj(x2, mods[l], gpre[1], w_in, l, batch=B, seq=S)
        o = _attention(proj, rel_bias_t, tables, gain2, n_heads_a=n_heads_a, n_heads_b=n_heads_b)
        x2 = _outproj(x2, o.reshape(B * S, d_mix), mods[l], gpost[1], w_out, l, seq=S)
        x2 = _ffn_sublayer(x2, mods[l], gpre[2], gpost[2], w2g, w2u, w2d, l, sub=2, seq=S)
    return x2.reshape(B, S, D)
```

```python
import functools
import math

import numpy as np
import jax
import jax.numpy as jnp
from jax import lax
from jax.experimental import pallas as pl
from jax.experimental.pallas import tpu as pltpu

F32 = jnp.float32
BF16 = jnp.bfloat16

HEAD_DIM = 128
MOBA_BLOCK = 256
MOBA_TOPK = 3
REL_BUCKETS = 32
REL_MAX_DIST = 128
N_SUB = 3
N_MOD = 3
EPS = 1e-6
SB_BLOCK = 256
SB_LOG_ZERO = -104.0

VMEM_LIMIT = 56 * 1024 * 1024


def _cparams(sem):
    return pltpu.CompilerParams(dimension_semantics=sem, vmem_limit_bytes=VMEM_LIMIT)


def _dot(a, b):
    return jnp.dot(a, b, preferred_element_type=F32)


def _dot_nt(a, b):
    return lax.dot_general(a, b, (((1,), (1,)), ((), ())), preferred_element_type=F32)


def _rms(x):
    return x * lax.rsqrt(jnp.mean(x * x, axis=-1, keepdims=True) + EPS)


def _ada_kernel(c_ref, w_ref, b_ref, o_ref):
    c = c_ref[...]
    ca = (c * jax.nn.sigmoid(c)).astype(BF16)
    o_ref[...] = _dot(ca, w_ref[...].astype(BF16)) + b_ref[...]


def _ada_mods(c_pad, ada_w, ada_b):
    L, D, N = ada_w.shape
    tn = min(1024, D)
    return pl.pallas_call(
        _ada_kernel,
        grid=(L, N // tn),
        in_specs=[
            pl.BlockSpec((8, D), lambda l, n: (0, 0)),
            pl.BlockSpec((None, D, tn), lambda l, n: (l, 0, n)),
            pl.BlockSpec((None, 1, tn), lambda l, n: (l, 0, n)),
        ],
        out_specs=pl.BlockSpec((None, 8, tn), lambda l, n: (l, 0, n)),
        out_shape=jax.ShapeDtypeStruct((L, 8, N), F32),
        compiler_params=_cparams(("parallel", "parallel")),
    )(c_pad, ada_w, ada_b.reshape(L, 1, N))


def _ffn_kernel(x_ref, mod_ref, gpre_ref, gpost_ref, wg_ref, wu_ref, wd_ref, o_ref,
                h_ref, acc_ref, *, sub, res_w):
    j = pl.program_id(1)

    @pl.when(j == 0)
    def _():
        y = _rms(x_ref[...]) * gpre_ref[...]
        shift = mod_ref[3 * sub:3 * sub + 1, :]
        scale = mod_ref[3 * sub + 1:3 * sub + 2, :]
        h_ref[...] = (y * (1.0 + scale) + shift).astype(BF16)
        acc_ref[...] = jnp.zeros_like(acc_ref)

    h = h_ref[...]
    g = _dot(h, wg_ref[...])
    u = _dot(h, wu_ref[...])
    a = (g * jax.nn.sigmoid(g) * u).astype(BF16)
    acc_ref[...] += _dot(a, wd_ref[...])

    @pl.when(j == pl.num_programs(1) - 1)
    def _():
        gate = mod_ref[3 * sub + 2:3 * sub + 3, :]
        yn = _rms(acc_ref[...]) * gpost_ref[...]
        o_ref[...] = x_ref[...] + res_w * gate * yn


def _ffn_sublayer(x2, mods_l, gpre, gpost, wg, wu, wd, l, *, sub, seq, tm=512, tf=512):
    M, D = x2.shape
    F = wg.shape[-1]
    tpb = seq // tm
    return pl.pallas_call(
        functools.partial(_ffn_kernel, sub=sub, res_w=0.5),
        grid=(M // tm, F // tf),
        in_specs=[
            pl.BlockSpec((tm, D), lambda i, j: (i, 0)),
            pl.BlockSpec((None, N_SUB * N_MOD, D), lambda i, j: (i // tpb, 0, 0)),
            pl.BlockSpec((1, D), lambda i, j: (0, 0)),
            pl.BlockSpec((1, D), lambda i, j: (0, 0)),
            pl.BlockSpec((None, D, tf), lambda i, j: (l, 0, j)),
            pl.BlockSpec((None, D, tf), lambda i, j: (l, 0, j)),
            pl.BlockSpec((None, tf, D), lambda i, j: (l, j, 0)),
        ],
        out_specs=pl.BlockSpec((tm, D), lambda i, j: (i, 0)),
        out_shape=jax.ShapeDtypeStruct((M, D), F32),
        scratch_shapes=[pltpu.VMEM((tm, D), BF16), pltpu.VMEM((tm, D), F32)],
        compiler_params=_cparams(("parallel", "arbitrary")),
    )(x2, mods_l, gpre, gpost, wg, wu, wd)


def _inproj_kernel(x_ref, mod_ref, gpre_ref, w_ref, o_ref, h_ref, *, heads_per_tile):
    @pl.when(pl.program_id(1) == 0)
    def _():
        y = _rms(x_ref[...]) * gpre_ref[...]
        shift = mod_ref[3:4, :]
        scale = mod_ref[4:5, :]
        h_ref[...] = (y * (1.0 + scale) + shift).astype(BF16)

    res = _dot(h_ref[...], w_ref[...])
    for hh in range(heads_per_tile):
        o_ref[hh] = res[:, hh * HEAD_DIM:(hh + 1) * HEAD_DIM].astype(BF16)


def _inproj(x2, mods_l, gpre, w_in, l, *, batch, seq, tm=512, tn=512):
    M, D = x2.shape
    N = w_in.shape[-1]
    tpb = seq // tm
    hpt = tn // HEAD_DIM
    return pl.pallas_call(
        functools.partial(_inproj_kernel, heads_per_tile=hpt),
        grid=(M // tm, N // tn),
        in_specs=[
            pl.BlockSpec((tm, D), lambda i, n: (i, 0)),
            pl.BlockSpec((None, N_SUB * N_MOD, D), lambda i, n: (i // tpb, 0, 0)),
            pl.BlockSpec((1, D), lambda i, n: (0, 0)),
            pl.BlockSpec((None, D, tn), lambda i, n: (l, 0, n)),
        ],
        out_specs=pl.BlockSpec((None, hpt, tm, HEAD_DIM), lambda i, n: (i // tpb, n, i % tpb, 0)),
        out_shape=jax.ShapeDtypeStruct((batch, N // HEAD_DIM, seq, HEAD_DIM), BF16),
        scratch_shapes=[pltpu.VMEM((tm, D), BF16)],
        compiler_params=_cparams(("parallel", "arbitrary")),
    )(x2, mods_l, gpre, w_in)


def _t5_bucket_np(dist):
    n = np.maximum(dist, 0)
    max_exact = REL_BUCKETS // 2
    nf = np.maximum(n, 1).astype(np.float32)
    large = max_exact + (np.log(nf / np.float32(max_exact)) / np.float32(math.log(REL_MAX_DIST / max_exact))
                         * np.float32(REL_BUCKETS - max_exact)).astype(np.int32)
    large = np.minimum(large, REL_BUCKETS - 1)
    return np.where(n < max_exact, n, large).astype(np.int32)


def _bias_table_kernel(rb_ref, bucket_ref, o_ref):
    h = pl.program_id(0)
    for t in range(2):
        bucket = bucket_ref[t]
        tab = jnp.zeros(bucket.shape, F32)
        for b in range(REL_BUCKETS):
            tab = jnp.where(bucket == b, rb_ref[h, b], tab)
        if t == 0:
            row = lax.broadcasted_iota(jnp.int32, bucket.shape, 0)
            col = lax.broadcasted_iota(jnp.int32, bucket.shape, 1)
            tab = jnp.where(row >= col, tab, -jnp.inf)
        o_ref[t] = tab


def _bias_tables(rel_bias_t):
    H = rel_bias_t.shape[0]
    r = np.arange(MOBA_BLOCK)
    rel = r[:, None] - r[None, :]
    buckets = jnp.asarray(np.stack([_t5_bucket_np(rel), _t5_bucket_np(rel + MOBA_BLOCK)]))
    return pl.pallas_call(
        _bias_table_kernel,
        grid=(H,),
        in_specs=[
            pl.BlockSpec(memory_space=pltpu.SMEM),
            pl.BlockSpec((2, MOBA_BLOCK, MOBA_BLOCK), lambda h: (0, 0, 0)),
        ],
        out_specs=pl.BlockSpec((None, 2, MOBA_BLOCK, MOBA_BLOCK), lambda h: (h, 0, 0, 0)),
        out_shape=jax.ShapeDtypeStruct((H, 2, MOBA_BLOCK, MOBA_BLOCK), F32),
        compiler_params=_cparams(("arbitrary",)),
    )(rel_bias_t, buckets)


def _head_out(acc, gain):
    return (_rms(acc) * gain).astype(BF16)


def _moba_head(h, rb_ref, q_ref, k_ref, v_ref, tab_ref, gain_ref, o_ref, s_ref, p_ref, *, nb):
    BLK = MOBA_BLOCK
    HALF = BLK // 2
    b_far = rb_ref[h, REL_BUCKETS - 1]
    scale = HEAD_DIM ** -0.5

    kf = k_ref[...].astype(F32).reshape(nb, BLK, HEAD_DIM)
    k_mean = jnp.mean(kf, axis=1)
    k_mean = jnp.concatenate([k_mean, jnp.zeros((HEAD_DIM - nb, HEAD_DIM), F32)], axis=0).astype(BF16)
    col = lax.broadcasted_iota(jnp.int32, (BLK, HEAD_DIM), 1)

    for i in range(nb):
        qi = q_ref[i * BLK:(i + 1) * BLK, :]
        if i > 0:
            gate = jnp.where(col < i, _dot_nt(qi, k_mean), -jnp.inf)
            rank = jnp.zeros((BLK, HEAD_DIM), jnp.int32)
            for jp in range(i):
                cj = gate[:, jp:jp + 1]
                beats = (cj > gate) | ((cj == gate) & (jp < col))
                rank = rank + beats.astype(jnp.int32)
            selw = jnp.where((rank < MOBA_TOPK) & (col < i), 0.0, -jnp.inf)

        m_run = None
        for j in range(i + 1):
            s = _dot_nt(qi, k_ref[j * BLK:(j + 1) * BLK, :]) * scale
            if j == i:
                s = s + tab_ref[0]
            elif j == i - 1:
                s = s + tab_ref[1] + selw[:, j:j + 1]
            else:
                s = s + (selw[:, j:j + 1] + b_far)
            s_ref[:, j * BLK:(j + 1) * BLK] = s
            mj = jnp.maximum(s[:, :HALF], s[:, HALF:])
            m_run = mj if m_run is None else jnp.maximum(m_run, mj)
        m = jnp.max(m_run, axis=-1, keepdims=True)

        l_run = jnp.zeros((BLK, HALF), F32)
        for j in range(i + 1):
            p = jnp.exp(s_ref[:, j * BLK:(j + 1) * BLK] - m)
            l_run = l_run + (p[:, :HALF] + p[:, HALF:])
            p_ref[:, j * BLK:(j + 1) * BLK] = p.astype(BF16)
        l = jnp.sum(l_run, axis=-1, keepdims=True)
        acc = _dot(p_ref[:, :(i + 1) * BLK], v_ref[:(i + 1) * BLK, :])
        o_ref[i * BLK:(i + 1) * BLK, :] = _head_out(acc / l, gain_ref[...])


def _sb_head(q_ref, k_ref, v_ref, gain_ref, o_ref, acc_ref, rem_ref, *, nb):
    BLK = SB_BLOCK
    scale = HEAD_DIM ** -0.5
    row = lax.broadcasted_iota(jnp.int32, (BLK, BLK), 0)
    col = lax.broadcasted_iota(jnp.int32, (BLK, BLK), 1)
    strict = col < row
    after = jnp.where(row > col, 1.0, 0.0).astype(BF16)

    def tail_sums(x):
        hi = x.astype(BF16)
        lo = (x - hi.astype(F32)).astype(BF16)
        return _dot(hi, after) + _dot(lo, after)

    def block_pair(qi, kj, vj, rem, diagonal):
        z = _dot_nt(qi, kj) * scale
        fail = jnp.maximum(z, 0.0) + jnp.log(1.0 + jnp.exp(-jnp.abs(z)))
        fail_m = jnp.where(strict, fail, 0.0) if diagonal else fail
        w = tail_sums(fail_m)
        log_a = z - fail - w
        if rem is not None:
            log_a = log_a + rem
        a = jnp.exp(log_a)
        if diagonal:
            a = jnp.where(strict, a, 0.0)
        spent = w[:, 0:1] + fail_m[:, 0:1]
        rem = -spent if rem is None else rem - spent
        return _dot(a.astype(BF16), vj), rem

    def static_blk(ref, j):
        return ref[j * BLK:(j + 1) * BLK, :]

    def finish(i, acc):
        o_ref[i * BLK:(i + 1) * BLK, :] = _head_out(acc, gain_ref[...])

    for i in range(nb):
        qi = static_blk(q_ref, i)
        acc, rem = block_pair(qi, static_blk(k_ref, i), static_blk(v_ref, i), None, True)
        if i > 0:
            add, rem = block_pair(qi, static_blk(k_ref, i - 1), static_blk(v_ref, i - 1), rem, False)
            acc = acc + add
        if i < 2:
            finish(i, acc)
        else:
            acc_ref[i] = acc
            rem_ref[i] = jnp.broadcast_to(rem, (BLK, HEAD_DIM))

    for i in range(2, nb):
        qi = static_blk(q_ref, i)

        def alive(rem):
            return (jnp.max(rem) > SB_LOG_ZERO).astype(jnp.int32)

        def cond(c):
            return (c[0] >= 0) & (c[1] > 0)

        def body(c):
            j, _, rem, acc = c
            start = pl.multiple_of(j * BLK, BLK)
            add, rem = block_pair(qi, k_ref[pl.ds(start, BLK), :], v_ref[pl.ds(start, BLK), :], rem, False)
            return j - 1, alive(rem), rem, acc + add

        rem0 = rem_ref[i][:, 0:1]
        _, _, _, acc = lax.while_loop(cond, body, (jnp.int32(i - 2), alive(rem0), rem0, acc_ref[i]))
        finish(i, acc)


def _attn_kernel(rb_ref, q_ref, k_ref, v_ref, tab_ref, gain_ref, o_ref, s_ref, p_ref, acc_ref, rem_ref,
                 *, nb, n_heads_a):
    g = pl.program_id(1)

    @pl.when(g < n_heads_a)
    def _():
        _moba_head(g, rb_ref, q_ref, k_ref, v_ref, tab_ref, gain_ref, o_ref, s_ref, p_ref, nb=nb)

    @pl.when(g >= n_heads_a)
    def _():
        _sb_head(q_ref, k_ref, v_ref, gain_ref, o_ref, acc_ref, rem_ref, nb=nb)


def _attention(proj, rel_bias_t, tables, gain2, *, n_heads_a, n_heads_b):
    B, _, S, _ = proj.shape
    nb = S // MOBA_BLOCK
    blk = (None, None, S, HEAD_DIM)

    def col_head(which):
        off_a = which * n_heads_a
        off_b = 3 * n_heads_a + which * n_heads_b - n_heads_a
        return lambda b, g: (b, jnp.where(g < n_heads_a, off_a + g, off_b + g), 0, 0)

    return pl.pallas_call(
        functools.partial(_attn_kernel, nb=nb, n_heads_a=n_heads_a),
        grid=(B, n_heads_a + n_heads_b),
        in_specs=[
            pl.BlockSpec(memory_space=pltpu.SMEM),
            pl.BlockSpec(blk, col_head(0)),
            pl.BlockSpec(blk, col_head(1)),
            pl.BlockSpec(blk, col_head(2)),
            pl.BlockSpec((None, 2, MOBA_BLOCK, MOBA_BLOCK),
                         lambda b, g: (jnp.minimum(g, n_heads_a - 1), 0, 0, 0)),
            pl.BlockSpec((1, HEAD_DIM), lambda b, g: (0, g)),
        ],
        out_specs=pl.BlockSpec((None, S, HEAD_DIM), lambda b, g: (b, 0, g)),
        out_shape=jax.ShapeDtypeStruct((B, S, (n_heads_a + n_heads_b) * HEAD_DIM), BF16),
        scratch_shapes=[
            pltpu.VMEM((MOBA_BLOCK, S), F32),
            pltpu.VMEM((MOBA_BLOCK, S), BF16),
            pltpu.VMEM((nb, SB_BLOCK, HEAD_DIM), F32),
            pltpu.VMEM((nb, SB_BLOCK, HEAD_DIM), F32),
        ],
        compiler_params=_cparams(("parallel", "parallel")),
    )(rel_bias_t, proj, proj, proj, tables, gain2)


def _outproj_kernel(x_ref, o_ref, mod_ref, gpost_ref, w_ref, out_ref):
    y = _dot(o_ref[...], w_ref[...])
    gate = mod_ref[5:6, :]
    out_ref[...] = x_ref[...] + gate * (_rms(y) * gpost_ref[...])


def _outproj(x2, o2, mods_l, gpost, w_out, l, *, seq, tm=512):
    M, D = x2.shape
    K = o2.shape[-1]
    tpb = seq // tm
    return pl.pallas_call(
        _outproj_kernel,
        grid=(M // tm,),
        in_specs=[
            pl.BlockSpec((tm, D), lambda i: (i, 0)),
            pl.BlockSpec((tm, K), lambda i: (i, 0)),
            pl.BlockSpec((None, N_SUB * N_MOD, D), lambda i: (i // tpb, 0, 0)),
            pl.BlockSpec((1, D), lambda i: (0, 0)),
            pl.BlockSpec((None, K, D), lambda i: (l, 0, 0)),
        ],
        out_specs=pl.BlockSpec((tm, D), lambda i: (i, 0)),
        out_shape=jax.ShapeDtypeStruct((M, D), F32),
        compiler_params=_cparams(("parallel",)),
    )(x2, o2, mods_l, gpost, w_out)


def kernel(x, c, ada_w, ada_b, norm_pre, norm_post, ffn1_w_gate, ffn1_w_up, ffn1_w_down,
           mix_w_in, mix_out_gain, mix_w_out, rel_bias, ffn2_w_gate, ffn2_w_up, ffn2_w_down):
    B, S, D = x.shape
    L = ada_w.shape[0]
    n_heads_a = rel_bias.shape[1]
    d_mix = mix_w_out.shape[1]
    n_heads_b = d_mix // HEAD_DIM - n_heads_a
    assert S % MOBA_BLOCK == 0 and mix_w_in.shape[-1] == 3 * d_mix

    c_pad = jnp.pad(c, ((0, 8 - B), (0, 0)))
    mods = _ada_mods(c_pad, ada_w, ada_b)[:, :B].reshape(L, B, N_SUB * N_MOD, D)
    rel_bias_t = rel_bias.T
    tables = _bias_tables(rel_bias_t)

    w1g, w1u, w1d = (w.astype(BF16) for w in (ffn1_w_gate, ffn1_w_up, ffn1_w_down))
    w2g, w2u, w2d = (w.astype(BF16) for w in (ffn2_w_gate, ffn2_w_up, ffn2_w_down))
    w_in = mix_w_in.astype(BF16)
    w_out = mix_w_out.astype(BF16)

    x2 = x.reshape(B * S, D)
    for l in range(L):
        gpre = norm_pre[l].reshape(N_SUB, 1, D)
        gpost = norm_post[l].reshape(N_SUB, 1, D)
        gain2 = mix_out_gain[l].reshape(1, d_mix)
        x2 = _ffn_sublayer(x2, mods[l], gpre[0], gpost[0], w1g, w1u, w1d, l, sub=0, seq=S)
        proj = _inproj(x2, mods[l], gpre[1], w_in, l, batch=B, seq=S)
        o = _attention(proj, rel_bias_t, tables, gain2, n_heads_a=n_heads_a, n_heads_b=n_heads_b)
        x2 = _outproj(x2, o.reshape(B * S, d_mix), mods[l], gpost[1], w_out, l, seq=S)
        x2 = _ffn_sublayer(x2, mods[l], gpre[2], gpost[2], w2g, w2u, w2d, l, sub=2, seq=S)
    return x2.reshape(B, S, D)
```

```python
import functools
import math

import numpy as np
import jax
import jax.numpy as jnp
from jax import lax
from jax.experimental import pallas as pl
from jax.experimental.pallas import tpu as pltpu

F32 = jnp.float32
BF16 = jnp.bfloat16

HEAD_DIM = 128
MOBA_BLOCK = 256
MOBA_TOPK = 3
REL_BUCKETS = 32
REL_MAX_DIST = 128
N_SUB = 3
N_MOD = 3
EPS = 1e-6
SB_BLOCK = 256
SB_LOG_ZERO = -104.0
SB_NEAR_BLOCKS = 3
LOG2E = math.log2(math.e)

VMEM_LIMIT = 62 * 1024 * 1024


def _cparams(sem):
    return pltpu.CompilerParams(dimension_semantics=sem, vmem_limit_bytes=VMEM_LIMIT)


def _dot(a, b):
    return jnp.dot(a, b, preferred_element_type=F32)


def _dot_nt(a, b):
    return lax.dot_general(a, b, (((1,), (1,)), ((), ())), preferred_element_type=F32)


def _rms(x):
    return x * lax.rsqrt(jnp.mean(x * x, axis=-1, keepdims=True) + EPS)


def _ada_kernel(c_ref, w_ref, b_ref, o_ref):
    c = c_ref[...]
    ca = (c * jax.nn.sigmoid(c)).astype(BF16)
    o_ref[...] = _dot(ca, w_ref[...].astype(BF16)) + b_ref[...]


def _ada_mods(c_pad, ada_w, ada_b):
    L, D, N = ada_w.shape
    tn = min(1024, D)
    return pl.pallas_call(
        _ada_kernel,
        grid=(L, N // tn),
        in_specs=[
            pl.BlockSpec((8, D), lambda l, n: (0, 0)),
            pl.BlockSpec((None, D, tn), lambda l, n: (l, 0, n)),
            pl.BlockSpec((None, 1, tn), lambda l, n: (l, 0, n)),
        ],
        out_specs=pl.BlockSpec((None, 8, tn), lambda l, n: (l, 0, n)),
        out_shape=jax.ShapeDtypeStruct((L, 8, N), F32),
        compiler_params=_cparams(("parallel", "parallel")),
    )(c_pad, ada_w, ada_b.reshape(L, 1, N))


def _ffn_kernel(x_ref, mod_ref, gpre_ref, gpost_ref, wg_ref, wu_ref, wd_ref, o_ref,
                h_ref, *, sub, res_w):
    j = pl.program_id(1)

    @pl.when(j == 0)
    def _():
        y = _rms(x_ref[...]) * gpre_ref[...]
        shift = mod_ref[3 * sub:3 * sub + 1, :]
        scale = mod_ref[3 * sub + 1:3 * sub + 2, :]
        h_ref[...] = (y * (1.0 + scale) + shift).astype(BF16)
        o_ref[...] = jnp.zeros_like(o_ref)

    h = h_ref[...]
    g = _dot(h, wg_ref[...])
    u = _dot(h, wu_ref[...])
    a = (g * jax.nn.sigmoid(g) * u).astype(BF16)
    o_ref[...] += _dot(a, wd_ref[...])

    @pl.when(j == pl.num_programs(1) - 1)
    def _():
        gate = mod_ref[3 * sub + 2:3 * sub + 3, :]
        yn = _rms(o_ref[...]) * gpost_ref[...]
        o_ref[...] = x_ref[...] + res_w * gate * yn


def _ffn_sublayer(x2, mods_l, gpre, gpost, wg, wu, wd, l, *, sub, seq, tm=1024, tf=512):
    M, D = x2.shape
    F = wg.shape[-1]
    tpb = seq // tm
    return pl.pallas_call(
        functools.partial(_ffn_kernel, sub=sub, res_w=0.5),
        grid=(M // tm, F // tf),
        in_specs=[
            pl.BlockSpec((tm, D), lambda i, j: (i, 0)),
            pl.BlockSpec((None, N_SUB * N_MOD, D), lambda i, j: (i // tpb, 0, 0)),
            pl.BlockSpec((1, D), lambda i, j: (0, 0)),
            pl.BlockSpec((1, D), lambda i, j: (0, 0)),
            pl.BlockSpec((None, D, tf), lambda i, j: (l, 0, j)),
            pl.BlockSpec((None, D, tf), lambda i, j: (l, 0, j)),
            pl.BlockSpec((None, tf, D), lambda i, j: (l, j, 0)),
        ],
        out_specs=pl.BlockSpec((tm, D), lambda i, j: (i, 0)),
        out_shape=jax.ShapeDtypeStruct((M, D), F32),
        scratch_shapes=[pltpu.VMEM((tm, D), BF16)],
        compiler_params=_cparams(("parallel", "arbitrary")),
    )(x2, mods_l, gpre, gpost, wg, wu, wd)


def _inproj_kernel(x_ref, mod_ref, gpre_ref, w_ref, o_ref, h_ref, *, heads_per_tile):
    @pl.when(pl.program_id(1) == 0)
    def _():
        y = _rms(x_ref[...]) * gpre_ref[...]
        shift = mod_ref[3:4, :]
        scale = mod_ref[4:5, :]
        h_ref[...] = (y * (1.0 + scale) + shift).astype(BF16)

    res = _dot(h_ref[...], w_ref[...])
    for hh in range(heads_per_tile):
        o_ref[hh] = res[:, hh * HEAD_DIM:(hh + 1) * HEAD_DIM].astype(BF16)


def _inproj(x2, mods_l, gpre, w_in, l, *, batch, seq, tm=1024, tn=1024):
    M, D = x2.shape
    N = w_in.shape[-1]
    tpb = seq // tm
    hpt = tn // HEAD_DIM
    return pl.pallas_call(
        functools.partial(_inproj_kernel, heads_per_tile=hpt),
        grid=(M // tm, N // tn),
        in_specs=[
            pl.BlockSpec((tm, D), lambda i, n: (i, 0)),
            pl.BlockSpec((None, N_SUB * N_MOD, D), lambda i, n: (i // tpb, 0, 0)),
            pl.BlockSpec((1, D), lambda i, n: (0, 0)),
            pl.BlockSpec((None, D, tn), lambda i, n: (l, 0, n)),
        ],
        out_specs=pl.BlockSpec((None, hpt, tm, HEAD_DIM), lambda i, n: (i // tpb, n, i % tpb, 0)),
        out_shape=jax.ShapeDtypeStruct((batch, N // HEAD_DIM, seq, HEAD_DIM), BF16),
        scratch_shapes=[pltpu.VMEM((tm, D), BF16)],
        compiler_params=_cparams(("parallel", "arbitrary")),
    )(x2, mods_l, gpre, w_in)


def _t5_bucket_np(dist):
    n = np.maximum(dist, 0)
    max_exact = REL_BUCKETS // 2
    nf = np.maximum(n, 1).astype(np.float32)
    large = max_exact + (np.log(nf / np.float32(max_exact)) / np.float32(math.log(REL_MAX_DIST / max_exact))
                         * np.float32(REL_BUCKETS - max_exact)).astype(np.int32)
    large = np.minimum(large, REL_BUCKETS - 1)
    return np.where(n < max_exact, n, large).astype(np.int32)


def _bias_table_kernel(rb_ref, bucket_ref, o_ref):
    h = pl.program_id(0)
    for t in range(2):
        bucket = bucket_ref[t]
        tab = jnp.zeros(bucket.shape, F32)
        for b in range(REL_BUCKETS):
            tab = jnp.where(bucket == b, rb_ref[h, b], tab)
        if t == 0:
            row = lax.broadcasted_iota(jnp.int32, bucket.shape, 0)
            col = lax.broadcasted_iota(jnp.int32, bucket.shape, 1)
            tab = jnp.where(row >= col, tab, -jnp.inf)
        o_ref[t] = tab * LOG2E


def _bias_tables(rel_bias_t):
    H = rel_bias_t.shape[0]
    r = np.arange(MOBA_BLOCK)
    rel = r[:, None] - r[None, :]
    buckets = jnp.asarray(np.stack([_t5_bucket_np(rel), _t5_bucket_np(rel + MOBA_BLOCK)]))
    return pl.pallas_call(
        _bias_table_kernel,
        grid=(H,),
        in_specs=[
            pl.BlockSpec(memory_space=pltpu.SMEM),
            pl.BlockSpec((2, MOBA_BLOCK, MOBA_BLOCK), lambda h: (0, 0, 0)),
        ],
        out_specs=pl.BlockSpec((None, 2, MOBA_BLOCK, MOBA_BLOCK), lambda h: (h, 0, 0, 0)),
        out_shape=jax.ShapeDtypeStruct((H, 2, MOBA_BLOCK, MOBA_BLOCK), F32),
        compiler_params=_cparams(("arbitrary",)),
    )(rel_bias_t, buckets)


def _head_out(acc, gain):
    return (_rms(acc) * gain).astype(BF16)


def _moba_head(h, rb_ref, q_ref, k_ref, v_ref, tab_ref, gain_ref, o_ref, s_ref, p_ref, *, nb):
    BLK = MOBA_BLOCK
    HALF = BLK // 2
    GATE_ROWS = 16
    assert nb <= GATE_ROWS
    b_far = rb_ref[h, REL_BUCKETS - 1] * LOG2E
    scale = HEAD_DIM ** -0.5 * LOG2E

    kf = k_ref[...].astype(F32).reshape(nb, BLK, HEAD_DIM)
    k_mean = jnp.mean(kf, axis=1)
    k_mean = jnp.concatenate([k_mean, jnp.zeros((GATE_ROWS - nb, HEAD_DIM), F32)], axis=0).astype(BF16)
    blk_id = lax.broadcasted_iota(jnp.int32, (GATE_ROWS, BLK), 0)

    def scores(i):
        qi = q_ref[i * BLK:(i + 1) * BLK, :]
        s_buf = s_ref.at[i % 2]
        if i > 0:
            gate = jnp.where(blk_id < i, _dot_nt(k_mean, qi), -jnp.inf)
            rank = jnp.zeros((GATE_ROWS, BLK), jnp.int32)
            for jp in range(i):
                cj = gate[jp:jp + 1, :]
                beats = (cj > gate) | ((cj == gate) & (jp < blk_id))
                rank = rank + beats.astype(jnp.int32)
            sel_t = jnp.where((rank < MOBA_TOPK) & (blk_id < i), 0.0, -jnp.inf)
            selw = jnp.concatenate([sel_t, jnp.zeros((HEAD_DIM - GATE_ROWS, BLK), F32)], axis=0).T

        m_run = None
        for j in range(i + 1):
            s = _dot_nt(qi, k_ref[j * BLK:(j + 1) * BLK, :]) * scale
            if j == i:
                s = s + tab_ref[0]
            elif j == i - 1:
                s = s + tab_ref[1] + selw[:, j:j + 1]
            else:
                s = s + (selw[:, j:j + 1] + b_far)
            s_buf[:, j * BLK:(j + 1) * BLK] = s
            mj = jnp.maximum(s[:, :HALF], s[:, HALF:])
            m_run = mj if m_run is None else jnp.maximum(m_run, mj)
        return jnp.max(m_run, axis=-1, keepdims=True)

    def output(i, m):
        s_buf = s_ref.at[i % 2]
        p_buf = p_ref.at[i % 2]
        l_run = jnp.zeros((BLK, HALF), F32)
        for j in range(i + 1):
            p = jnp.exp2(s_buf[:, j * BLK:(j + 1) * BLK] - m)
            l_run = l_run + (p[:, :HALF] + p[:, HALF:])
            p_buf[:, j * BLK:(j + 1) * BLK] = p.astype(BF16)
        l = jnp.sum(l_run, axis=-1, keepdims=True)
        acc = _dot(p_buf[:, :(i + 1) * BLK], v_ref[:(i + 1) * BLK, :])
        o_ref[i * BLK:(i + 1) * BLK, :] = _head_out(acc / l, gain_ref[...])

    m = scores(0)
    for i in range(nb):
        m_next = scores(i + 1) if i + 1 < nb else None
        output(i, m)
        m = m_next


def _sb_head(q_ref, k_ref, v_ref, gain_ref, o_ref, acc_ref, rem_ref, *, nb):
    BLK = SB_BLOCK
    scale = HEAD_DIM ** -0.5 * LOG2E
    row = lax.broadcasted_iota(jnp.int32, (BLK, BLK), 0)
    col = lax.broadcasted_iota(jnp.int32, (BLK, BLK), 1)
    strict = col < row
    after = jnp.where(row > col, 1.0, 0.0).astype(BF16)

    def tail_sums(x):
        hi = x.astype(BF16)
        lo = (x - hi.astype(F32)).astype(BF16)
        return _dot(hi, after) + _dot(lo, after)

    def strip_logits(qi, k_strip, diagonal):
        n = k_strip.shape[0] // BLK
        z = _dot_nt(qi, k_strip) * scale
        t = jnp.exp2(z)
        e = jnp.minimum(t, 1.0 / t)
        fail = jnp.maximum(z, 0.0) + jnp.log(1.0 + e) * LOG2E
        parts = []
        for b in range(n):
            cols = slice(b * BLK, (b + 1) * BLK)
            f_b = fail[:, cols]
            f_m = jnp.where(strict, f_b, 0.0) if diagonal and b == n - 1 else f_b
            w = tail_sums(f_m)
            parts.append((z[:, cols] - f_b - w, w[:, 0:1] + f_m[:, 0:1]))
        return parts

    def strip_output(parts, v_strip, rem, diagonal):
        n = len(parts)
        weights = [None] * n
        for b in reversed(range(n)):
            log_a, spent = parts[b]
            a = jnp.exp2(log_a if rem is None else log_a + rem)
            if diagonal and b == n - 1:
                a = jnp.where(strict, a, 0.0)
            weights[b] = a.astype(BF16)
            rem = -spent if rem is None else rem - spent
        a_all = weights[0] if n == 1 else jnp.concatenate(weights, axis=1)
        return _dot(a_all, v_strip), rem

    def strip(qi, k_strip, v_strip, rem, diagonal):
        return strip_output(strip_logits(qi, k_strip, diagonal), v_strip, rem, diagonal)

    def alive(rem):
        return (jnp.max(rem) > SB_LOG_ZERO * LOG2E).astype(jnp.int32)

    def near_keys(i):
        return slice(max(0, i - (SB_NEAR_BLOCKS - 1)) * BLK, (i + 1) * BLK)

    pending = []
    parts = strip_logits(q_ref[0:BLK, :], k_ref[near_keys(0), :], True)
    for i in range(nb):
        if i + 1 < nb:
            parts_next = strip_logits(q_ref[(i + 1) * BLK:(i + 2) * BLK, :], k_ref[near_keys(i + 1), :], True)
        acc, rem = strip_output(parts, v_ref[near_keys(i), :], None, True)
        if near_keys(i).start == 0:
            o_ref[i * BLK:(i + 1) * BLK, :] = _head_out(acc, gain_ref[...])
        else:
            acc_ref[i] = acc
            rem_ref[i] = jnp.broadcast_to(rem, (BLK, HEAD_DIM))
            pending.append((i, near_keys(i).start // BLK - 1, alive(rem)))
        parts = parts_next

    for i, next_blk, alive0 in pending:
        qi = q_ref[i * BLK:(i + 1) * BLK, :]

        def cond(c):
            return (c[0] >= 0) & (c[1] > 0)

        def body(c, qi=qi):
            j, _, rem, acc = c
            keys = pl.ds(pl.multiple_of(j * BLK, BLK), BLK)
            add, rem = strip(qi, k_ref[keys, :], v_ref[keys, :], rem, False)
            return j - 1, alive(rem), rem, acc + add

        _, _, _, acc = lax.while_loop(cond, body, (jnp.int32(next_blk), alive0, rem_ref[i][:, 0:1], acc_ref[i]))
        o_ref[i * BLK:(i + 1) * BLK, :] = _head_out(acc, gain_ref[...])


def _attn_kernel(rb_ref, q_ref, k_ref, v_ref, tab_ref, gain_ref, o_ref, s_ref, p_ref, acc_ref, rem_ref,
                 *, nb, n_heads_a):
    g = pl.program_id(1)

    @pl.when(g < n_heads_a)
    def _():
        _moba_head(g, rb_ref, q_ref, k_ref, v_ref, tab_ref, gain_ref, o_ref, s_ref, p_ref, nb=nb)

    @pl.when(g >= n_heads_a)
    def _():
        _sb_head(q_ref, k_ref, v_ref, gain_ref, o_ref, acc_ref, rem_ref, nb=nb)


def _attention(proj, rel_bias_t, tables, gain2, *, n_heads_a, n_heads_b):
    B, _, S, _ = proj.shape
    nb = S // MOBA_BLOCK
    blk = (None, None, S, HEAD_DIM)

    def col_head(which):
        off_a = which * n_heads_a
        off_b = 3 * n_heads_a + which * n_heads_b - n_heads_a
        return lambda b, g: (b, jnp.where(g < n_heads_a, off_a + g, off_b + g), 0, 0)

    return pl.pallas_call(
        functools.partial(_attn_kernel, nb=nb, n_heads_a=n_heads_a),
        grid=(B, n_heads_a + n_heads_b),
        in_specs=[
            pl.BlockSpec(memory_space=pltpu.SMEM),
            pl.BlockSpec(blk, col_head(0)),
            pl.BlockSpec(blk, col_head(1)),
            pl.BlockSpec(blk, col_head(2)),
            pl.BlockSpec((None, 2, MOBA_BLOCK, MOBA_BLOCK),
                         lambda b, g: (jnp.minimum(g, n_heads_a - 1), 0, 0, 0)),
            pl.BlockSpec((1, HEAD_DIM), lambda b, g: (0, g)),
        ],
        out_specs=pl.BlockSpec((None, S, HEAD_DIM), lambda b, g: (b, 0, g)),
        out_shape=jax.ShapeDtypeStruct((B, S, (n_heads_a + n_heads_b) * HEAD_DIM), BF16),
        scratch_shapes=[
            pltpu.VMEM((2, MOBA_BLOCK, S), F32),
            pltpu.VMEM((2, MOBA_BLOCK, S), BF16),
            pltpu.VMEM((nb, SB_BLOCK, HEAD_DIM), F32),
            pltpu.VMEM((nb, SB_BLOCK, HEAD_DIM), F32),
        ],
        compiler_params=_cparams(("parallel", "parallel")),
    )(rel_bias_t, proj, proj, proj, tables, gain2)


def _outproj_kernel(x_ref, o_ref, mod_ref, gpost_ref, w_ref, out_ref):
    y = _dot(o_ref[...], w_ref[...])
    gate = mod_ref[5:6, :]
    out_ref[...] = x_ref[...] + gate * (_rms(y) * gpost_ref[...])


def _outproj(x2, o2, mods_l, gpost, w_out, l, *, seq, tm=512):
    M, D = x2.shape
    K = o2.shape[-1]
    tpb = seq // tm
    return pl.pallas_call(
        _outproj_kernel,
        grid=(M // tm,),
        in_specs=[
            pl.BlockSpec((tm, D), lambda i: (i, 0)),
            pl.BlockSpec((tm, K), lambda i: (i, 0)),
            pl.BlockSpec((None, N_SUB * N_MOD, D), lambda i: (i // tpb, 0, 0)),
            pl.BlockSpec((1, D), lambda i: (0, 0)),
            pl.BlockSpec((None, K, D), lambda i: (l, 0, 0)),
        ],
        out_specs=pl.BlockSpec((tm, D), lambda i: (i, 0)),
        out_shape=jax.ShapeDtypeStruct((M, D), F32),
        compiler_params=_cparams(("parallel",)),
    )(x2, o2, mods_l, gpost, w_out)


def kernel(x, c, ada_w, ada_b, norm_pre, norm_post, ffn1_w_gate, ffn1_w_up, ffn1_w_down,
           mix_w_in, mix_out_gain, mix_w_out, rel_bias, ffn2_w_gate, ffn2_w_up, ffn2_w_down):
    B, S, D = x.shape
    L = ada_w.shape[0]
    n_heads_a = rel_bias.shape[1]
    d_mix = mix_w_out.shape[1]
    n_heads_b = d_mix // HEAD_DIM - n_heads_a
    assert S % MOBA_BLOCK == 0 and mix_w_in.shape[-1] == 3 * d_mix

    c_pad = jnp.pad(c, ((0, 8 - B), (0, 0)))
    mods = _ada_mods(c_pad, ada_w, ada_b)[:, :B].reshape(L, B, N_SUB * N_MOD, D)
    rel_bias_t = rel_bias.T
    tables = _bias_tables(rel_bias_t)

    w1g, w1u, w1d = (w.astype(BF16) for w in (ffn1_w_gate, ffn1_w_up, ffn1_w_down))
    w2g, w2u, w2d = (w.astype(BF16) for w in (ffn2_w_gate, ffn2_w_up, ffn2_w_down))
    w_in = mix_w_in.astype(BF16)
    w_out = mix_w_out.astype(BF16)

    x2 = x.reshape(B * S, D)
    for l in range(L):
        gpre = norm_pre[l].reshape(N_SUB, 1, D)
        gpost = norm_post[l].reshape(N_SUB, 1, D)
        gain2 = mix_out_gain[l].reshape(1, d_mix)
        x2 = _ffn_sublayer(x2, mods[l], gpre[0], gpost[0], w1g, w1u, w1d, l, sub=0, seq=S)
        proj = _inproj(x2, mods[l], gpre[1], w_in, l, batch=B, seq=S)
        o = _attention(proj, rel_bias_t, tables, gain2, n_heads_a=n_heads_a, n_heads_b=n_heads_b)
        x2 = _outproj(x2, o.reshape(B * S, d_mix), mods[l], gpost[1], w_out, l, seq=S)
        x2 = _ffn_sublayer(x2, mods[l], gpre[2], gpost[2], w2g, w2u, w2d, l, sub=2, seq=S)
    return x2.reshape(B, S, D)
```

```python
import functools
import math

import numpy as np
import jax
import jax.numpy as jnp
from jax import lax
from jax.experimental import pallas as pl
from jax.experimental.pallas import tpu as pltpu

F32 = jnp.float32
BF16 = jnp.bfloat16

HEAD_DIM = 128
MOBA_BLOCK = 256
MOBA_TOPK = 3
REL_BUCKETS = 32
REL_MAX_DIST = 128
N_SUB = 3
N_MOD = 3
EPS = 1e-6
SB_BLOCK = 256
SB_LOG_ZERO = -104.0
SB_NEAR_BLOCKS = 3
LOG2E = math.log2(math.e)
QK_SCALE = HEAD_DIM ** -0.5 * LOG2E
ROW_CHUNK = 16
ROW_UNROLL = 8

VMEM_LIMIT = 62 * 1024 * 1024


def _cparams(sem):
    return pltpu.CompilerParams(dimension_semantics=sem, vmem_limit_bytes=VMEM_LIMIT)


def _dot(a, b):
    return jnp.dot(a, b, preferred_element_type=F32)


def _dot_nt(a, b):
    return lax.dot_general(a, b, (((1,), (1,)), ((), ())), preferred_element_type=F32)


def _rms(x):
    return x * lax.rsqrt(jnp.mean(x * x, axis=-1, keepdims=True) + EPS)


def _row_chunks(n_rows, body):
    def step(r, carry):
        body(pl.ds(pl.multiple_of(r * ROW_CHUNK, ROW_CHUNK), ROW_CHUNK))
        return carry

    lax.fori_loop(0, n_rows // ROW_CHUNK, step, 0, unroll=ROW_UNROLL)


def _modulated_prenorm(x_ref, gpre_ref, mod_ref, sub, h_ref, zero_ref=None):
    gain = gpre_ref[...] * (1.0 + mod_ref[3 * sub + 1:3 * sub + 2, :])
    shift = mod_ref[3 * sub:3 * sub + 1, :]

    def body(rows):
        h_ref[rows, :] = (_rms(x_ref[rows, :]) * gain + shift).astype(BF16)
        if zero_ref is not None:
            zero_ref[rows, :] = jnp.zeros((ROW_CHUNK, zero_ref.shape[1]), zero_ref.dtype)

    _row_chunks(x_ref.shape[0], body)


def _gated_postnorm_residual(x_ref, y_ref, gpost_ref, mod_ref, sub, res_w, o_ref):
    coef = (res_w * mod_ref[3 * sub + 2:3 * sub + 3, :]) * gpost_ref[...]

    for r in range(0, x_ref.shape[0], ROW_CHUNK):
        rows = slice(r, r + ROW_CHUNK)
        o_ref[rows, :] = x_ref[rows, :] + _rms(y_ref[rows, :]) * coef


def _ada_kernel(c_ref, w_ref, b_ref, o_ref):
    c = c_ref[...]
    ca = (c * jax.nn.sigmoid(c)).astype(BF16)
    o_ref[...] = _dot(ca, w_ref[...].astype(BF16)) + b_ref[...]


def _ada_mods(c_pad, ada_w, ada_b):
    L, D, N = ada_w.shape
    tn = min(1024, D)
    return pl.pallas_call(
        _ada_kernel,
        grid=(L, N // tn),
        in_specs=[
            pl.BlockSpec((8, D), lambda l, n: (0, 0)),
            pl.BlockSpec((None, D, tn), lambda l, n: (l, 0, n)),
            pl.BlockSpec((None, 1, tn), lambda l, n: (l, 0, n)),
        ],
        out_specs=pl.BlockSpec((None, 8, tn), lambda l, n: (l, 0, n)),
        out_shape=jax.ShapeDtypeStruct((L, 8, N), F32),
        compiler_params=_cparams(("parallel", "parallel")),
    )(c_pad, ada_w, ada_b.reshape(L, 1, N))


def _ffn_kernel(x_ref, mod_ref, gpre_ref, gpost_ref, wg_ref, wu_ref, wd_ref, o_ref,
                h_ref, *, sub, res_w):
    j = pl.program_id(1)

    @pl.when(j == 0)
    def _():
        _modulated_prenorm(x_ref, gpre_ref, mod_ref, sub, h_ref, zero_ref=o_ref)

    h = h_ref[...]
    g = _dot(h, wg_ref[...])
    u = _dot(h, wu_ref[...])
    a = (g * jax.nn.sigmoid(g) * u).astype(BF16)
    o_ref[...] += _dot(a, wd_ref[...])

    @pl.when(j == pl.num_programs(1) - 1)
    def _():
        _gated_postnorm_residual(x_ref, o_ref, gpost_ref, mod_ref, sub, res_w, o_ref)


def _ffn_sublayer(x2, mods_l, gpre, gpost, wg, wu, wd, l, *, sub, seq, tm=1024, tf=512):
    M, D = x2.shape
    F = wg.shape[-1]
    tpb = seq // tm
    return pl.pallas_call(
        functools.partial(_ffn_kernel, sub=sub, res_w=0.5),
        grid=(M // tm, F // tf),
        in_specs=[
            pl.BlockSpec((tm, D), lambda i, j: (i, 0)),
            pl.BlockSpec((None, N_SUB * N_MOD, D), lambda i, j: (i // tpb, 0, 0)),
            pl.BlockSpec((1, D), lambda i, j: (0, 0)),
            pl.BlockSpec((1, D), lambda i, j: (0, 0)),
            pl.BlockSpec((None, D, tf), lambda i, j: (l, 0, j)),
            pl.BlockSpec((None, D, tf), lambda i, j: (l, 0, j)),
            pl.BlockSpec((None, tf, D), lambda i, j: (l, j, 0)),
        ],
        out_specs=pl.BlockSpec((tm, D), lambda i, j: (i, 0)),
        out_shape=jax.ShapeDtypeStruct((M, D), F32),
        scratch_shapes=[pltpu.VMEM((tm, D), BF16)],
        compiler_params=_cparams(("parallel", "arbitrary")),
    )(x2, mods_l, gpre, gpost, wg, wu, wd)


def _inproj_kernel(x_ref, mod_ref, gpre_ref, w_ref, o_ref, h_ref, *, heads_per_tile, tiles_per_section):
    n = pl.program_id(1)

    @pl.when(n == 0)
    def _():
        _modulated_prenorm(x_ref, gpre_ref, mod_ref, 1, h_ref)

    res = _dot(h_ref[...], w_ref[...])
    res = res * jnp.where((n // tiles_per_section) % 3 == 0, QK_SCALE, 1.0)
    for hh in range(heads_per_tile):
        o_ref[hh] = res[:, hh * HEAD_DIM:(hh + 1) * HEAD_DIM].astype(BF16)


def _inproj(x2, mods_l, gpre, w_in, l, *, batch, seq, tm=1024, tn=1024):
    M, D = x2.shape
    N = w_in.shape[-1]
    section = N // 6
    tn = min(tn, section)
    assert section % tn == 0
    tpb = seq // tm
    hpt = tn // HEAD_DIM
    return pl.pallas_call(
        functools.partial(_inproj_kernel, heads_per_tile=hpt, tiles_per_section=section // tn),
        grid=(M // tm, N // tn),
        in_specs=[
            pl.BlockSpec((tm, D), lambda i, n: (i, 0)),
            pl.BlockSpec((None, N_SUB * N_MOD, D), lambda i, n: (i // tpb, 0, 0)),
            pl.BlockSpec((1, D), lambda i, n: (0, 0)),
            pl.BlockSpec((None, D, tn), lambda i, n: (l, 0, n)),
        ],
        out_specs=pl.BlockSpec((None, hpt, tm, HEAD_DIM), lambda i, n: (i // tpb, n, i % tpb, 0)),
        out_shape=jax.ShapeDtypeStruct((batch, N // HEAD_DIM, seq, HEAD_DIM), BF16),
        scratch_shapes=[pltpu.VMEM((tm, D), BF16)],
        compiler_params=_cparams(("parallel", "arbitrary")),
    )(x2, mods_l, gpre, w_in)


def _t5_bucket_np(dist):
    n = np.maximum(dist, 0)
    max_exact = REL_BUCKETS // 2
    nf = np.maximum(n, 1).astype(np.float32)
    large = max_exact + (np.log(nf / np.float32(max_exact)) / np.float32(math.log(REL_MAX_DIST / max_exact))
                         * np.float32(REL_BUCKETS - max_exact)).astype(np.int32)
    large = np.minimum(large, REL_BUCKETS - 1)
    return np.where(n < max_exact, n, large).astype(np.int32)


def _bias_table_kernel(rb_ref, bucket_ref, o_ref):
    h = pl.program_id(0)
    for t in range(2):
        bucket = bucket_ref[t]
        tab = jnp.zeros(bucket.shape, F32)
        for b in range(REL_BUCKETS):
            tab = jnp.where(bucket == b, rb_ref[h, b], tab)
        if t == 0:
            row = lax.broadcasted_iota(jnp.int32, bucket.shape, 0)
            col = lax.broadcasted_iota(jnp.int32, bucket.shape, 1)
            tab = jnp.where(row >= col, tab, -jnp.inf)
        o_ref[t] = tab * LOG2E


def _bias_tables(rel_bias_t):
    H = rel_bias_t.shape[0]
    r = np.arange(MOBA_BLOCK)
    rel = r[:, None] - r[None, :]
    buckets = jnp.asarray(np.stack([_t5_bucket_np(rel), _t5_bucket_np(rel + MOBA_BLOCK)]))
    return pl.pallas_call(
        _bias_table_kernel,
        grid=(H,),
        in_specs=[
            pl.BlockSpec(memory_space=pltpu.SMEM),
            pl.BlockSpec((2, MOBA_BLOCK, MOBA_BLOCK), lambda h: (0, 0, 0)),
        ],
        out_specs=pl.BlockSpec((None, 2, MOBA_BLOCK, MOBA_BLOCK), lambda h: (h, 0, 0, 0)),
        out_shape=jax.ShapeDtypeStruct((H, 2, MOBA_BLOCK, MOBA_BLOCK), F32),
        compiler_params=_cparams(("arbitrary",)),
    )(rel_bias_t, buckets)


def _head_out(acc, gain):
    return (_rms(acc) * gain).astype(BF16)


def _moba_stages(h, rb_ref, q_ref, k_ref, v_ref, tab_ref, gain_ref, o_ref, s_ref, p_ref, v1_ref, *, nb):
    BLK = MOBA_BLOCK
    HALF = BLK // 2
    GATE_ROWS = 16
    assert nb <= GATE_ROWS
    b_far = rb_ref[h, REL_BUCKETS - 1] * LOG2E

    v1_ref[:, :HEAD_DIM] = v_ref[...]
    v1_ref[:, HEAD_DIM:] = jnp.ones((nb * BLK, HEAD_DIM), BF16)

    kf = k_ref[...].astype(F32).reshape(nb, BLK, HEAD_DIM)
    k_mean = jnp.mean(kf, axis=1)
    k_mean = jnp.concatenate([k_mean, jnp.zeros((GATE_ROWS - nb, HEAD_DIM), F32)], axis=0).astype(BF16)
    blk_id = lax.broadcasted_iota(jnp.int32, (GATE_ROWS, BLK), 0)

    def scores(i):
        qi = q_ref[i * BLK:(i + 1) * BLK, :]
        s_buf = s_ref.at[i % 2]
        if i > 0:
            gate = jnp.where(blk_id < i, _dot_nt(k_mean, qi), -jnp.inf)
            rank = jnp.zeros((GATE_ROWS, BLK), jnp.int32)
            for jp in range(i):
                cj = gate[jp:jp + 1, :]
                beats = (cj > gate) | ((cj == gate) & (jp < blk_id))
                rank = rank + beats.astype(jnp.int32)
            sel_t = jnp.where((rank < MOBA_TOPK) & (blk_id < i), 0.0, -jnp.inf)
            selw = jnp.concatenate([sel_t, jnp.zeros((HEAD_DIM - GATE_ROWS, BLK), F32)], axis=0).T

        m_run = None
        for j in range(i + 1):
            s = _dot_nt(qi, k_ref[j * BLK:(j + 1) * BLK, :])
            if j == i:
                s = s + tab_ref[0]
            elif j == i - 1:
                s = s + tab_ref[1] + selw[:, j:j + 1]
            else:
                s = s + (selw[:, j:j + 1] + b_far)
            s_buf[:, j * BLK:(j + 1) * BLK] = s
            mj = jnp.maximum(s[:, :HALF], s[:, HALF:])
            m_run = mj if m_run is None else jnp.maximum(m_run, mj)
        return jnp.max(m_run, axis=-1, keepdims=True)

    def output(i, m):
        s_buf = s_ref.at[i % 2]
        p_buf = p_ref.at[i % 2]
        for j in range(i + 1):
            p_buf[:, j * BLK:(j + 1) * BLK] = jnp.exp2(s_buf[:, j * BLK:(j + 1) * BLK] - m).astype(BF16)
        acc = _dot(p_buf[:, :(i + 1) * BLK], v1_ref[:(i + 1) * BLK, :])
        o_ref[i * BLK:(i + 1) * BLK, :] = _head_out(acc[:, :HEAD_DIM] / acc[:, HEAD_DIM:HEAD_DIM + 1],
                                                    gain_ref[...])

    return scores, output


def _sb_stages(q_ref, k_ref, v_ref, gain_ref, o_ref, acc_ref, rem_ref, *, nb):
    BLK = SB_BLOCK
    row = lax.broadcasted_iota(jnp.int32, (BLK, BLK), 0)
    col = lax.broadcasted_iota(jnp.int32, (BLK, BLK), 1)
    strict = col < row
    after = jnp.where(row > col, 1.0, 0.0).astype(BF16)

    def tail_sums(x):
        return _dot(x.astype(BF16), after)

    def strip_logits(qi, k_strip, diagonal):
        n = k_strip.shape[0] // BLK
        z = _dot_nt(qi, k_strip)
        t = jnp.exp2(z)
        e = jnp.minimum(t, 1.0 / t)
        fail = jnp.maximum(z, 0.0) + jnp.log2(1.0 + e)
        parts = []
        for b in range(n):
            cols = slice(b * BLK, (b + 1) * BLK)
            f_b = fail[:, cols]
            f_m = jnp.where(strict, f_b, 0.0) if diagonal and b == n - 1 else f_b
            w = tail_sums(f_m)
            parts.append((z[:, cols] - f_b - w, w[:, 0:1] + f_m[:, 0:1]))
        return parts

    def strip_output(parts, v_strip, rem, diagonal):
        n = len(parts)
        weights = [None] * n
        for b in reversed(range(n)):
            log_a, spent = parts[b]
            a = jnp.exp2(log_a if rem is None else log_a + rem)
            if diagonal and b == n - 1:
                a = jnp.where(strict, a, 0.0)
            weights[b] = a.astype(BF16)
            rem = -spent if rem is None else rem - spent
        a_all = weights[0] if n == 1 else jnp.concatenate(weights, axis=1)
        return _dot(a_all, v_strip), rem

    def strip(qi, k_strip, v_strip, rem, diagonal):
        return strip_output(strip_logits(qi, k_strip, diagonal), v_strip, rem, diagonal)

    def alive(rem):
        return (jnp.max(rem) > SB_LOG_ZERO * LOG2E).astype(jnp.int32)

    def near_keys(i):
        return slice(max(0, i - (SB_NEAR_BLOCKS - 1)) * BLK, (i + 1) * BLK)

    pending = []

    def logits(i):
        return strip_logits(q_ref[i * BLK:(i + 1) * BLK, :], k_ref[near_keys(i), :], True)

    def output(i, parts):
        acc, rem = strip_output(parts, v_ref[near_keys(i), :], None, True)
        if near_keys(i).start == 0:
            o_ref[i * BLK:(i + 1) * BLK, :] = _head_out(acc, gain_ref[...])
        else:
            acc_ref[i] = acc
            rem_ref[i] = jnp.broadcast_to(rem, (BLK, HEAD_DIM))
            pending.append((i, near_keys(i).start // BLK - 1, alive(rem)))

    def older():
        for i, next_blk, alive0 in pending:
            qi = q_ref[i * BLK:(i + 1) * BLK, :]

            def trip(c, n, qi=qi):
                j, _, rem, acc = c
                keys = pl.ds(pl.multiple_of((j - (n - 1)) * BLK, BLK), n * BLK)
                add, rem = strip(qi, k_ref[keys, :], v_ref[keys, :], rem, False)
                return j - n, alive(rem), rem, acc + add

            state = (jnp.int32(next_blk), alive0, rem_ref[i][:, 0:1], acc_ref[i])
            if next_blk >= 1:
                state = lax.while_loop(lambda c: (c[0] >= 1) & (c[1] > 0), functools.partial(trip, n=2), state)
            state = lax.while_loop(lambda c: (c[0] >= 0) & (c[1] > 0), functools.partial(trip, n=1), state)
            o_ref[i * BLK:(i + 1) * BLK, :] = _head_out(state[3], gain_ref[...])

    return logits, output, older


def _attn_kernel(rb_ref, qa_ref, ka_ref, va_ref, qb_ref, kb_ref, vb_ref, tab_ref, gain_a_ref, gain_b_ref,
                 oa_ref, ob_ref, s_ref, p_ref, v1_ref, acc_ref, rem_ref, *, nb):
    g = pl.program_id(1)
    a_scores, a_output = _moba_stages(g, rb_ref, qa_ref, ka_ref, va_ref, tab_ref, gain_a_ref, oa_ref,
                                      s_ref, p_ref, v1_ref, nb=nb)
    b_logits, b_output, b_older = _sb_stages(qb_ref, kb_ref, vb_ref, gain_b_ref, ob_ref, acc_ref, rem_ref, nb=nb)
    m, parts = a_scores(0), b_logits(0)
    for i in range(nb):
        if i + 1 < nb:
            m_next, parts_next = a_scores(i + 1), b_logits(i + 1)
        a_output(i, m)
        b_output(i, parts)
        m, parts = m_next, parts_next
    b_older()


def _attention(proj, rel_bias_t, tables, gain2, *, n_heads):
    B, _, S, _ = proj.shape
    nb = S // MOBA_BLOCK
    blk = (None, None, S, HEAD_DIM)

    def col_head(which):
        return pl.BlockSpec(blk, lambda b, g: (b, which * n_heads + g, 0, 0))

    out_spec = pl.BlockSpec((None, S, HEAD_DIM), lambda b, g: (b, 0, g))
    out_shape = jax.ShapeDtypeStruct((B, S, n_heads * HEAD_DIM), BF16)
    return pl.pallas_call(
        functools.partial(_attn_kernel, nb=nb),
        grid=(B, n_heads),
        in_specs=[
            pl.BlockSpec(memory_space=pltpu.SMEM),
            *[col_head(which) for which in range(6)],
            pl.BlockSpec((None, 2, MOBA_BLOCK, MOBA_BLOCK), lambda b, g: (g, 0, 0, 0)),
            pl.BlockSpec((1, HEAD_DIM), lambda b, g: (0, g)),
            pl.BlockSpec((1, HEAD_DIM), lambda b, g: (0, n_heads + g)),
        ],
        out_specs=[out_spec, out_spec],
        out_shape=[out_shape, out_shape],
        scratch_shapes=[
            pltpu.VMEM((2, MOBA_BLOCK, S), F32),
            pltpu.VMEM((2, MOBA_BLOCK, S), BF16),
            pltpu.VMEM((S, 2 * HEAD_DIM), BF16),
            pltpu.VMEM((nb, SB_BLOCK, HEAD_DIM), F32),
            pltpu.VMEM((nb, SB_BLOCK, HEAD_DIM), F32),
        ],
        compiler_params=_cparams(("parallel", "parallel")),
    )(rel_bias_t, *([proj] * 6), tables, gain2, gain2)


def _outproj_kernel(x_ref, oa_ref, ob_ref, mod_ref, gpost_ref, wa_ref, wb_ref, out_ref):
    out_ref[...] = _dot(oa_ref[...], wa_ref[...]) + _dot(ob_ref[...], wb_ref[...])
    _gated_postnorm_residual(x_ref, out_ref, gpost_ref, mod_ref, 1, 1.0, out_ref)


def _outproj(x2, oa2, ob2, mods_l, gpost, w_out, l, *, seq, tm=512):
    M, D = x2.shape
    K = oa2.shape[-1]
    tpb = seq // tm
    return pl.pallas_call(
        _outproj_kernel,
        grid=(M // tm,),
        in_specs=[
            pl.BlockSpec((tm, D), lambda i: (i, 0)),
            pl.BlockSpec((tm, K), lambda i: (i, 0)),
            pl.BlockSpec((tm, K), lambda i: (i, 0)),
            pl.BlockSpec((None, N_SUB * N_MOD, D), lambda i: (i // tpb, 0, 0)),
            pl.BlockSpec((1, D), lambda i: (0, 0)),
            pl.BlockSpec((None, K, D), lambda i: (l, 0, 0)),
            pl.BlockSpec((None, K, D), lambda i: (l, 1, 0)),
        ],
        out_specs=pl.BlockSpec((tm, D), lambda i: (i, 0)),
        out_shape=jax.ShapeDtypeStruct((M, D), F32),
        compiler_params=_cparams(("parallel",)),
    )(x2, oa2, ob2, mods_l, gpost, w_out, w_out)


def kernel(x, c, ada_w, ada_b, norm_pre, norm_post, ffn1_w_gate, ffn1_w_up, ffn1_w_down,
           mix_w_in, mix_out_gain, mix_w_out, rel_bias, ffn2_w_gate, ffn2_w_up, ffn2_w_down):
    B, S, D = x.shape
    L = ada_w.shape[0]
    n_heads = rel_bias.shape[1]
    d_mix = mix_w_out.shape[1]
    d_grp = n_heads * HEAD_DIM
    assert S % MOBA_BLOCK == 0 and d_mix == 2 * d_grp and mix_w_in.shape[-1] == 3 * d_mix

    c_pad = jnp.pad(c, ((0, 8 - B), (0, 0)))
    mods = _ada_mods(c_pad, ada_w, ada_b)[:, :B].reshape(L, B, N_SUB * N_MOD, D)
    rel_bias_t = rel_bias.T
    tables = _bias_tables(rel_bias_t)

    w1g, w1u, w1d = (w.astype(BF16) for w in (ffn1_w_gate, ffn1_w_up, ffn1_w_down))
    w2g, w2u, w2d = (w.astype(BF16) for w in (ffn2_w_gate, ffn2_w_up, ffn2_w_down))
    w_in = mix_w_in.astype(BF16)
    w_out = mix_w_out.astype(BF16)

    x2 = x.reshape(B * S, D)
    for l in range(L):
        gpre = norm_pre[l].reshape(N_SUB, 1, D)
        gpost = norm_post[l].reshape(N_SUB, 1, D)
        gain2 = mix_out_gain[l].reshape(1, d_mix)
        x2 = _ffn_sublayer(x2, mods[l], gpre[0], gpost[0], w1g, w1u, w1d, l, sub=0, seq=S)
        proj = _inproj(x2, mods[l], gpre[1], w_in, l, batch=B, seq=S)
        o_a, o_b = _attention(proj, rel_bias_t, tables, gain2, n_heads=n_heads)
        x2 = _outproj(x2, o_a.reshape(B * S, d_grp), o_b.reshape(B * S, d_grp), mods[l], gpost[1], w_out, l, seq=S)
        x2 = _ffn_sublayer(x2, mods[l], gpre[2], gpost[2], w2g, w2u, w2d, l, sub=2, seq=S)
    return x2.reshape(B, S, D)
```

```python
import functools
import math

import numpy as np
import jax
import jax.numpy as jnp
from jax import lax
from jax.experimental import pallas as pl
from jax.experimental.pallas import tpu as pltpu

F32 = jnp.float32
BF16 = jnp.bfloat16

HEAD_DIM = 128
MOBA_BLOCK = 256
MOBA_TOPK = 3
REL_BUCKETS = 32
REL_MAX_DIST = 128
N_SUB = 3
N_MOD = 3
EPS = 1e-6
SB_BLOCK = 256
SB_LOG_ZERO = -104.0
SB_NEAR_BLOCKS = 3
LOG2E = math.log2(math.e)
FFN_DOWN_COLS = 512
QK_SCALE = HEAD_DIM ** -0.5 * LOG2E
ROW_CHUNK = 16
ROW_UNROLL = 8

VMEM_LIMIT = 62 * 1024 * 1024


def _cparams(sem):
    return pltpu.CompilerParams(dimension_semantics=sem, vmem_limit_bytes=VMEM_LIMIT)


def _dot(a, b):
    return jnp.dot(a, b, preferred_element_type=F32)


def _dot_nt(a, b):
    return lax.dot_general(a, b, (((1,), (1,)), ((), ())), preferred_element_type=F32)


def _rms(x):
    return x * lax.rsqrt(jnp.mean(x * x, axis=-1, keepdims=True) + EPS)


def _row_chunks(n_rows, body):
    def step(r, carry):
        body(pl.ds(pl.multiple_of(r * ROW_CHUNK, ROW_CHUNK), ROW_CHUNK))
        return carry

    lax.fori_loop(0, n_rows // ROW_CHUNK, step, 0, unroll=ROW_UNROLL)


def _modulated_prenorm(x_ref, gpre_ref, mod_ref, sub, h_ref, zero_ref=None):
    gain = gpre_ref[...] * (1.0 + mod_ref[3 * sub + 1:3 * sub + 2, :])
    shift = mod_ref[3 * sub:3 * sub + 1, :]

    def body(rows):
        h_ref[rows, :] = (_rms(x_ref[rows, :]) * gain + shift).astype(BF16)
        if zero_ref is not None:
            zero_ref[rows, :] = jnp.zeros((ROW_CHUNK, zero_ref.shape[1]), zero_ref.dtype)

    _row_chunks(x_ref.shape[0], body)


def _gated_postnorm_residual(x_ref, y_ref, gpost_ref, mod_ref, sub, res_w, o_ref):
    coef = (res_w * mod_ref[3 * sub + 2:3 * sub + 3, :]) * gpost_ref[...]

    for r in range(0, x_ref.shape[0], ROW_CHUNK):
        rows = slice(r, r + ROW_CHUNK)
        o_ref[rows, :] = x_ref[rows, :] + _rms(y_ref[rows, :]) * coef


def _ada_kernel(c_ref, w_ref, b_ref, o_ref):
    c = c_ref[...]
    ca = (c * jax.nn.sigmoid(c)).astype(BF16)
    o_ref[...] = _dot(ca, w_ref[...].astype(BF16)) + b_ref[...]


def _ada_mods(c_pad, ada_w, ada_b):
    L, D, N = ada_w.shape
    tn = min(1024, D)
    return pl.pallas_call(
        _ada_kernel,
        grid=(L, N // tn),
        in_specs=[
            pl.BlockSpec((8, D), lambda l, n: (0, 0)),
            pl.BlockSpec((None, D, tn), lambda l, n: (l, 0, n)),
            pl.BlockSpec((None, 1, tn), lambda l, n: (l, 0, n)),
        ],
        out_specs=pl.BlockSpec((None, 8, tn), lambda l, n: (l, 0, n)),
        out_shape=jax.ShapeDtypeStruct((L, 8, N), F32),
        compiler_params=_cparams(("parallel", "parallel")),
    )(c_pad, ada_w, ada_b.reshape(L, 1, N))


def _ffn_kernel(x_ref, mod_ref, gpre_ref, gpost_ref, wg_ref, wu_ref, wd_ref, *rest, sub, res_w, n_next):
    next_f32, o_ref, next_bf16, h_ref = rest[:n_next], rest[n_next], rest[n_next + 1:-1], rest[-1]
    j = pl.program_id(1)

    @pl.when(j == 0)
    def _():
        _modulated_prenorm(x_ref, gpre_ref, mod_ref, sub, h_ref, zero_ref=o_ref)

    for src, dst in zip(next_f32, next_bf16):
        dst[...] = src[...].astype(BF16)

    h = h_ref[...]
    g = _dot(h, wg_ref[...])
    u = _dot(h, wu_ref[...])
    a = (g * jax.nn.sigmoid(g) * u).astype(BF16)
    for c in range(0, o_ref.shape[1], FFN_DOWN_COLS):
        cols = slice(c, c + FFN_DOWN_COLS)
        o_ref[:, cols] += _dot(a, wd_ref[:, cols])

    @pl.when(j == pl.num_programs(1) - 1)
    def _():
        _gated_postnorm_residual(x_ref, o_ref, gpost_ref, mod_ref, sub, res_w, o_ref)


def _cast_rows(n_rows, n_steps):
    for r in range(16, n_rows + 1, 16):
        if n_rows % r == 0 and n_rows // r <= n_steps:
            return r
    raise ValueError((n_rows, n_steps))


def _ffn_sublayer(x2, mods_l, gpre, gpost, w, next_w, *, sub, seq, tm=1024, tf=512):
    wg, wu, wd = w
    M, D = x2.shape
    F = wg.shape[-1]
    tpb = seq // tm
    n_i, n_j = M // tm, F // tf
    in_specs = [
        pl.BlockSpec((tm, D), lambda i, j: (i, 0)),
        pl.BlockSpec((None, N_SUB * N_MOD, D), lambda i, j: (i // tpb, 0, 0)),
        pl.BlockSpec((1, D), lambda i, j: (0, 0)),
        pl.BlockSpec((1, D), lambda i, j: (0, 0)),
        pl.BlockSpec((D, tf), lambda i, j: (0, j)),
        pl.BlockSpec((D, tf), lambda i, j: (0, j)),
        pl.BlockSpec((tf, D), lambda i, j: (j, 0)),
    ]
    out_specs = [pl.BlockSpec((tm, D), lambda i, j: (i, 0))]
    out_shape = [jax.ShapeDtypeStruct((M, D), F32)]
    operands = [x2, mods_l, gpre, gpost, wg, wu, wd]
    n_next = 0
    if next_w is not None:
        l_next, mats = next_w
        n_next = len(mats)
        for m in mats:
            _, rows, cols = m.shape
            r = _cast_rows(rows, n_i * n_j)
            last = rows // r - 1
            in_specs.append(pl.BlockSpec(
                (None, r, cols), lambda i, j, last=last: (l_next, jnp.minimum(i * n_j + j, last), 0)))
            out_specs.append(pl.BlockSpec(
                (r, cols), lambda i, j, last=last: (jnp.minimum(i * n_j + j, last), 0)))
            out_shape.append(jax.ShapeDtypeStruct((rows, cols), BF16))
            operands.append(m)
    outs = pl.pallas_call(
        functools.partial(_ffn_kernel, sub=sub, res_w=0.5, n_next=n_next),
        grid=(n_i, n_j),
        in_specs=in_specs,
        out_specs=out_specs,
        out_shape=out_shape,
        scratch_shapes=[pltpu.VMEM((tm, D), BF16)],
        compiler_params=_cparams(("arbitrary", "arbitrary")),
    )(*operands)
    return outs[0], tuple(outs[1:])


def _inproj_kernel(x_ref, mod_ref, gpre_ref, w_ref, o_ref, h_ref, *, heads_per_tile, tiles_per_section):
    n = pl.program_id(1)

    @pl.when(n == 0)
    def _():
        _modulated_prenorm(x_ref, gpre_ref, mod_ref, 1, h_ref)

    res = _dot(h_ref[...], w_ref[...])
    res = res * jnp.where((n // tiles_per_section) % 3 == 0, QK_SCALE, 1.0)
    for hh in range(heads_per_tile):
        o_ref[hh] = res[:, hh * HEAD_DIM:(hh + 1) * HEAD_DIM].astype(BF16)


def _inproj(x2, mods_l, gpre, w_in, *, batch, seq, tm=1024, tn=1024):
    M, D = x2.shape
    N = w_in.shape[-1]
    section = N // 6
    tn = min(tn, section)
    assert section % tn == 0
    tpb = seq // tm
    hpt = tn // HEAD_DIM
    return pl.pallas_call(
        functools.partial(_inproj_kernel, heads_per_tile=hpt, tiles_per_section=section // tn),
        grid=(M // tm, N // tn),
        in_specs=[
            pl.BlockSpec((tm, D), lambda i, n: (i, 0)),
            pl.BlockSpec((None, N_SUB * N_MOD, D), lambda i, n: (i // tpb, 0, 0)),
            pl.BlockSpec((1, D), lambda i, n: (0, 0)),
            pl.BlockSpec((D, tn), lambda i, n: (0, n)),
        ],
        out_specs=pl.BlockSpec((None, hpt, tm, HEAD_DIM), lambda i, n: (i // tpb, n, i % tpb, 0)),
        out_shape=jax.ShapeDtypeStruct((batch, N // HEAD_DIM, seq, HEAD_DIM), BF16),
        scratch_shapes=[pltpu.VMEM((tm, D), BF16)],
        compiler_params=_cparams(("parallel", "arbitrary")),
    )(x2, mods_l, gpre, w_in)


def _t5_bucket_np(dist):
    n = np.maximum(dist, 0)
    max_exact = REL_BUCKETS // 2
    nf = np.maximum(n, 1).astype(np.float32)
    large = max_exact + (np.log(nf / np.float32(max_exact)) / np.float32(math.log(REL_MAX_DIST / max_exact))
                         * np.float32(REL_BUCKETS - max_exact)).astype(np.int32)
    large = np.minimum(large, REL_BUCKETS - 1)
    return np.where(n < max_exact, n, large).astype(np.int32)


def _bias_table_kernel(rb_ref, bucket_ref, o_ref):
    h = pl.program_id(0)
    for t in range(2):
        bucket = bucket_ref[t]
        tab = jnp.zeros(bucket.shape, F32)
        for b in range(REL_BUCKETS):
            tab = jnp.where(bucket == b, rb_ref[h, b], tab)
        if t == 0:
            row = lax.broadcasted_iota(jnp.int32, bucket.shape, 0)
            col = lax.broadcasted_iota(jnp.int32, bucket.shape, 1)
            tab = jnp.where(row >= col, tab, -jnp.inf)
        o_ref[t] = tab * LOG2E


def _bias_tables(rel_bias_t):
    H = rel_bias_t.shape[0]
    r = np.arange(MOBA_BLOCK)
    rel = r[:, None] - r[None, :]
    buckets = jnp.asarray(np.stack([_t5_bucket_np(rel), _t5_bucket_np(rel + MOBA_BLOCK)]))
    return pl.pallas_call(
        _bias_table_kernel,
        grid=(H,),
        in_specs=[
            pl.BlockSpec(memory_space=pltpu.SMEM),
            pl.BlockSpec((2, MOBA_BLOCK, MOBA_BLOCK), lambda h: (0, 0, 0)),
        ],
        out_specs=pl.BlockSpec((None, 2, MOBA_BLOCK, MOBA_BLOCK), lambda h: (h, 0, 0, 0)),
        out_shape=jax.ShapeDtypeStruct((H, 2, MOBA_BLOCK, MOBA_BLOCK), F32),
        compiler_params=_cparams(("arbitrary",)),
    )(rel_bias_t, buckets)


def _head_out(acc, gain):
    return (_rms(acc) * gain).astype(BF16)


def _moba_stages(h, rb_ref, q_ref, k_ref, v_ref, tab_ref, gain_ref, o_ref, s_ref, p_ref, v1_ref, *, nb):
    BLK = MOBA_BLOCK
    HALF = BLK // 2
    GATE_ROWS = 16
    assert nb <= GATE_ROWS
    b_far = rb_ref[h, REL_BUCKETS - 1] * LOG2E

    v1_ref[:, :HEAD_DIM] = v_ref[...]
    v1_ref[:, HEAD_DIM:] = jnp.ones((nb * BLK, HEAD_DIM), BF16)

    kf = k_ref[...].astype(F32).reshape(nb, BLK, HEAD_DIM)
    k_mean = jnp.mean(kf, axis=1)
    k_mean = jnp.concatenate([k_mean, jnp.zeros((GATE_ROWS - nb, HEAD_DIM), F32)], axis=0).astype(BF16)
    blk_id = lax.broadcasted_iota(jnp.int32, (GATE_ROWS, BLK), 0)

    def scores(i):
        qi = q_ref[i * BLK:(i + 1) * BLK, :]
        s_buf = s_ref.at[i % 2]
        if i > 0:
            gate = jnp.where(blk_id < i, _dot_nt(k_mean, qi), -jnp.inf)
            rank = jnp.zeros((GATE_ROWS, BLK), jnp.int32)
            for jp in range(i):
                cj = gate[jp:jp + 1, :]
                beats = (cj > gate) | ((cj == gate) & (jp < blk_id))
                rank = rank + beats.astype(jnp.int32)
            sel_t = jnp.where((rank < MOBA_TOPK) & (blk_id < i), 0.0, -jnp.inf)
            selw = jnp.concatenate([sel_t, jnp.zeros((HEAD_DIM - GATE_ROWS, BLK), F32)], axis=0).T

        m_run = None
        for j in range(i + 1):
            s = _dot_nt(qi, k_ref[j * BLK:(j + 1) * BLK, :])
            if j == i:
                s = s + tab_ref[0]
            elif j == i - 1:
                s = s + tab_ref[1] + selw[:, j:j + 1]
            else:
                s = s + (selw[:, j:j + 1] + b_far)
            s_buf[:, j * BLK:(j + 1) * BLK] = s
            mj = jnp.maximum(s[:, :HALF], s[:, HALF:])
            m_run = mj if m_run is None else jnp.maximum(m_run, mj)
        return jnp.max(m_run, axis=-1, keepdims=True)

    def output(i, m):
        s_buf = s_ref.at[i % 2]
        p_buf = p_ref.at[i % 2]
        for j in range(i + 1):
            p_buf[:, j * BLK:(j + 1) * BLK] = jnp.exp2(s_buf[:, j * BLK:(j + 1) * BLK] - m).astype(BF16)
        acc = _dot(p_buf[:, :(i + 1) * BLK], v1_ref[:(i + 1) * BLK, :])
        o_ref[i * BLK:(i + 1) * BLK, :] = _head_out(acc[:, :HEAD_DIM] / acc[:, HEAD_DIM:HEAD_DIM + 1],
                                                    gain_ref[...])

    return scores, output


def _sb_stages(q_ref, k_ref, v_ref, gain_ref, o_ref, acc_ref, rem_ref, *, nb):
    BLK = SB_BLOCK
    row = lax.broadcasted_iota(jnp.int32, (BLK, BLK), 0)
    col = lax.broadcasted_iota(jnp.int32, (BLK, BLK), 1)
    strict = col < row
    after = jnp.where(row > col, 1.0, 0.0).astype(BF16)

    def tail_sums(x):
        return _dot(x.astype(BF16), after)

    def strip_logits(qi, k_strip, diagonal):
        n = k_strip.shape[0] // BLK
        z = _dot_nt(qi, k_strip)
        t = jnp.exp2(z)
        e = jnp.minimum(t, 1.0 / t)
        fail = jnp.maximum(z, 0.0) + jnp.log2(1.0 + e)
        parts = []
        for b in range(n):
            cols = slice(b * BLK, (b + 1) * BLK)
            f_b = fail[:, cols]
            f_m = jnp.where(strict, f_b, 0.0) if diagonal and b == n - 1 else f_b
            w = tail_sums(f_m)
            parts.append((z[:, cols] - f_b - w, w[:, 0:1] + f_m[:, 0:1]))
        return parts

    def strip_output(parts, v_strip, rem, diagonal):
        n = len(parts)
        weights = [None] * n
        for b in reversed(range(n)):
            log_a, spent = parts[b]
            a = jnp.exp2(log_a if rem is None else log_a + rem)
            if diagonal and b == n - 1:
                a = jnp.where(strict, a, 0.0)
            weights[b] = a.astype(BF16)
            rem = -spent if rem is None else rem - spent
        a_all = weights[0] if n == 1 else jnp.concatenate(weights, axis=1)
        return _dot(a_all, v_strip), rem

    def strip(qi, k_strip, v_strip, rem, diagonal):
        return strip_output(strip_logits(qi, k_strip, diagonal), v_strip, rem, diagonal)

    def alive(rem):
        return (jnp.max(rem) > SB_LOG_ZERO * LOG2E).astype(jnp.int32)

    def near_keys(i):
        return slice(max(0, i - (SB_NEAR_BLOCKS - 1)) * BLK, (i + 1) * BLK)

    pending = []

    def logits(i):
        return strip_logits(q_ref[i * BLK:(i + 1) * BLK, :], k_ref[near_keys(i), :], True)

    def output(i, parts):
        acc, rem = strip_output(parts, v_ref[near_keys(i), :], None, True)
        if near_keys(i).start == 0:
            o_ref[i * BLK:(i + 1) * BLK, :] = _head_out(acc, gain_ref[...])
        else:
            acc_ref[i] = acc
            rem_ref[i] = jnp.broadcast_to(rem, (BLK, HEAD_DIM))
            pending.append((i, near_keys(i).start // BLK - 1, alive(rem)))

    def older():
        for i, next_blk, alive0 in pending:
            qi = q_ref[i * BLK:(i + 1) * BLK, :]

            def trip(c, n, qi=qi):
                j, _, rem, acc = c
                keys = pl.ds(pl.multiple_of((j - (n - 1)) * BLK, BLK), n * BLK)
                add, rem = strip(qi, k_ref[keys, :], v_ref[keys, :], rem, False)
                return j - n, alive(rem), rem, acc + add

            state = (jnp.int32(next_blk), alive0, rem_ref[i][:, 0:1], acc_ref[i])
            if next_blk >= 1:
                state = lax.while_loop(lambda c: (c[0] >= 1) & (c[1] > 0), functools.partial(trip, n=2), state)
            state = lax.while_loop(lambda c: (c[0] >= 0) & (c[1] > 0), functools.partial(trip, n=1), state)
            o_ref[i * BLK:(i + 1) * BLK, :] = _head_out(state[3], gain_ref[...])

    return logits, output, older


def _attn_kernel(rb_ref, qa_ref, ka_ref, va_ref, qb_ref, kb_ref, vb_ref, tab_ref, gain_a_ref, gain_b_ref,
                 oa_ref, ob_ref, s_ref, p_ref, v1_ref, acc_ref, rem_ref, *, nb):
    g = pl.program_id(1)
    a_scores, a_output = _moba_stages(g, rb_ref, qa_ref, ka_ref, va_ref, tab_ref, gain_a_ref, oa_ref,
                                      s_ref, p_ref, v1_ref, nb=nb)
    b_logits, b_output, b_older = _sb_stages(qb_ref, kb_ref, vb_ref, gain_b_ref, ob_ref, acc_ref, rem_ref, nb=nb)
    m, parts = a_scores(0), b_logits(0)
    for i in range(nb):
        if i + 1 < nb:
            m_next, parts_next = a_scores(i + 1), b_logits(i + 1)
        a_output(i, m)
        b_output(i, parts)
        m, parts = m_next, parts_next
    b_older()


def _attention(proj, rel_bias_t, tables, gain2, *, n_heads):
    B, _, S, _ = proj.shape
    nb = S // MOBA_BLOCK
    blk = (None, None, S, HEAD_DIM)

    def col_head(which):
        return pl.BlockSpec(blk, lambda b, g: (b, which * n_heads + g, 0, 0))

    out_spec = pl.BlockSpec((None, S, HEAD_DIM), lambda b, g: (b, 0, g))
    out_shape = jax.ShapeDtypeStruct((B, S, n_heads * HEAD_DIM), BF16)
    return pl.pallas_call(
        functools.partial(_attn_kernel, nb=nb),
        grid=(B, n_heads),
        in_specs=[
            pl.BlockSpec(memory_space=pltpu.SMEM),
            *[col_head(which) for which in range(6)],
            pl.BlockSpec((None, 2, MOBA_BLOCK, MOBA_BLOCK), lambda b, g: (g, 0, 0, 0)),
            pl.BlockSpec((1, HEAD_DIM), lambda b, g: (0, g)),
            pl.BlockSpec((1, HEAD_DIM), lambda b, g: (0, n_heads + g)),
        ],
        out_specs=[out_spec, out_spec],
        out_shape=[out_shape, out_shape],
        scratch_shapes=[
            pltpu.VMEM((2, MOBA_BLOCK, S), F32),
            pltpu.VMEM((2, MOBA_BLOCK, S), BF16),
            pltpu.VMEM((S, 2 * HEAD_DIM), BF16),
            pltpu.VMEM((nb, SB_BLOCK, HEAD_DIM), F32),
            pltpu.VMEM((nb, SB_BLOCK, HEAD_DIM), F32),
        ],
        compiler_params=_cparams(("parallel", "parallel")),
    )(rel_bias_t, *([proj] * 6), tables, gain2, gain2)


def _outproj_kernel(x_ref, oa_ref, ob_ref, mod_ref, gpost_ref, wa_ref, wb_ref, out_ref):
    out_ref[...] = _dot(oa_ref[...], wa_ref[...]) + _dot(ob_ref[...], wb_ref[...])
    _gated_postnorm_residual(x_ref, out_ref, gpost_ref, mod_ref, 1, 1.0, out_ref)


def _outproj(x2, oa2, ob2, mods_l, gpost, w_out, *, seq, tm=512):
    M, D = x2.shape
    K = oa2.shape[-1]
    tpb = seq // tm
    return pl.pallas_call(
        _outproj_kernel,
        grid=(M // tm,),
        in_specs=[
            pl.BlockSpec((tm, D), lambda i: (i, 0)),
            pl.BlockSpec((tm, K), lambda i: (i, 0)),
            pl.BlockSpec((tm, K), lambda i: (i, 0)),
            pl.BlockSpec((None, N_SUB * N_MOD, D), lambda i: (i // tpb, 0, 0)),
            pl.BlockSpec((1, D), lambda i: (0, 0)),
            pl.BlockSpec((K, D), lambda i: (0, 0)),
            pl.BlockSpec((K, D), lambda i: (1, 0)),
        ],
        out_specs=pl.BlockSpec((tm, D), lambda i: (i, 0)),
        out_shape=jax.ShapeDtypeStruct((M, D), F32),
        compiler_params=_cparams(("parallel",)),
    )(x2, oa2, ob2, mods_l, gpost, w_out, w_out)


def kernel(x, c, ada_w, ada_b, norm_pre, norm_post, ffn1_w_gate, ffn1_w_up, ffn1_w_down,
           mix_w_in, mix_out_gain, mix_w_out, rel_bias, ffn2_w_gate, ffn2_w_up, ffn2_w_down):
    B, S, D = x.shape
    L = ada_w.shape[0]
    n_heads = rel_bias.shape[1]
    d_mix = mix_w_out.shape[1]
    d_grp = n_heads * HEAD_DIM
    assert S % MOBA_BLOCK == 0 and d_mix == 2 * d_grp and mix_w_in.shape[-1] == 3 * d_mix

    c_pad = jnp.pad(c, ((0, 8 - B), (0, 0)))
    mods = _ada_mods(c_pad, ada_w, ada_b)[:, :B].reshape(L, B, N_SUB * N_MOD, D)
    rel_bias_t = rel_bias.T
    tables = _bias_tables(rel_bias_t)

    ffn1 = [ffn1_w_gate, ffn1_w_up, ffn1_w_down]
    ffn2 = [ffn2_w_gate, ffn2_w_up, ffn2_w_down]
    w1 = tuple(w[0].astype(BF16) for w in ffn1)
    w_in = [mix_w_in[l].astype(BF16) for l in range(L)]
    w_out = [mix_w_out[l].astype(BF16) for l in range(L)]

    x2 = x.reshape(B * S, D)
    for l in range(L):
        gpre = norm_pre[l].reshape(N_SUB, 1, D)
        gpost = norm_post[l].reshape(N_SUB, 1, D)
        gain2 = mix_out_gain[l].reshape(1, d_mix)
        x2, w2 = _ffn_sublayer(x2, mods[l], gpre[0], gpost[0], w1, (l, ffn2), sub=0, seq=S)
        proj = _inproj(x2, mods[l], gpre[1], w_in[l], batch=B, seq=S)
        o_a, o_b = _attention(proj, rel_bias_t, tables, gain2, n_heads=n_heads)
        x2 = _outproj(x2, o_a.reshape(B * S, d_grp), o_b.reshape(B * S, d_grp), mods[l], gpost[1], w_out[l],
                      seq=S)
        x2, w1 = _ffn_sublayer(x2, mods[l], gpre[2], gpost[2], w2, (l + 1, ffn1) if l + 1 < L else None,
                               sub=2, seq=S)
    return x2.reshape(B, S, D)
```

```python
import functools
import math

import numpy as np
import jax
import jax.numpy as jnp
from jax import lax
from jax.experimental import pallas as pl
from jax.experimental.pallas import tpu as pltpu

F32 = jnp.float32
BF16 = jnp.bfloat16

HEAD_DIM = 128
MOBA_BLOCK = 256
MOBA_TOPK = 3
REL_BUCKETS = 32
REL_MAX_DIST = 128
N_SUB = 3
N_MOD = 3
EPS = 1e-6
SB_BLOCK = 256
SB_LOG_ZERO = -104.0
SB_NEAR_BLOCKS = 3
LOG2E = math.log2(math.e)
FFN_DOWN_COLS = 512
QK_SCALE = HEAD_DIM ** -0.5 * LOG2E
ROW_CHUNK = 16
ROW_UNROLL = 8

VMEM_LIMIT = 62 * 1024 * 1024


def _cparams(sem):
    return pltpu.CompilerParams(dimension_semantics=sem, vmem_limit_bytes=VMEM_LIMIT)


def _dot(a, b):
    return jnp.dot(a, b, preferred_element_type=F32)


def _dot_nt(a, b):
    return lax.dot_general(a, b, (((1,), (1,)), ((), ())), preferred_element_type=F32)


def _rms(x):
    return x * lax.rsqrt(jnp.mean(x * x, axis=-1, keepdims=True) + EPS)


def _row_chunks(n_rows, body):
    def step(r, carry):
        body(pl.ds(pl.multiple_of(r * ROW_CHUNK, ROW_CHUNK), ROW_CHUNK))
        return carry

    lax.fori_loop(0, n_rows // ROW_CHUNK, step, 0, unroll=ROW_UNROLL)


def _modulated_prenorm(x_ref, gpre_ref, mod_ref, sub, h_ref, zero_ref=None):
    gain = gpre_ref[...] * (1.0 + mod_ref[3 * sub + 1:3 * sub + 2, :])
    shift = mod_ref[3 * sub:3 * sub + 1, :]

    def body(rows):
        h_ref[rows, :] = (_rms(x_ref[rows, :]) * gain + shift).astype(BF16)
        if zero_ref is not None:
            zero_ref[rows, :] = jnp.zeros((ROW_CHUNK, zero_ref.shape[1]), zero_ref.dtype)

    _row_chunks(x_ref.shape[0], body)


def _gated_postnorm_residual(x_ref, y_ref, gpost_ref, mod_ref, sub, res_w, o_ref):
    coef = (res_w * mod_ref[3 * sub + 2:3 * sub + 3, :]) * gpost_ref[...]

    for r in range(0, x_ref.shape[0], ROW_CHUNK):
        rows = slice(r, r + ROW_CHUNK)
        o_ref[rows, :] = x_ref[rows, :] + _rms(y_ref[rows, :]) * coef


def _ada_kernel(c_ref, w_ref, b_ref, *rest, n_next):
    next_f32, o_ref, next_bf16 = rest[:n_next], rest[n_next], rest[n_next + 1:]
    c = c_ref[...]
    ca = (c * jax.nn.sigmoid(c)).astype(BF16)
    o_ref[...] = _dot(ca, w_ref[...].astype(BF16)) + b_ref[...]
    _cast_blocks(next_f32, next_bf16)


def _ada_mods(c_pad, ada_w, ada_b, next_w):
    L, D, N = ada_w.shape
    tn = min(1024, D)
    n_n = N // tn
    in_specs = [
        pl.BlockSpec((8, D), lambda l, n: (0, 0)),
        pl.BlockSpec((None, D, tn), lambda l, n: (l, 0, n)),
        pl.BlockSpec((None, 1, tn), lambda l, n: (l, 0, n)),
    ]
    out_specs = [pl.BlockSpec((None, 8, tn), lambda l, n: (l, 0, n))]
    out_shape = [jax.ShapeDtypeStruct((L, 8, N), F32)]
    operands = [c_pad, ada_w, ada_b.reshape(L, 1, N)]
    n_next = _add_cast_ahead(next_w, L * n_n, lambda l, n: l * n_n + n, in_specs, out_specs, out_shape, operands)
    outs = pl.pallas_call(
        functools.partial(_ada_kernel, n_next=n_next),
        grid=(L, n_n),
        in_specs=in_specs,
        out_specs=out_specs,
        out_shape=out_shape,
        compiler_params=_cparams(("arbitrary", "arbitrary")),
    )(*operands)
    return outs[0], tuple(outs[1:])


def _ffn_kernel(x_ref, mod_ref, gpre_ref, gpost_ref, wg_ref, wu_ref, wd_ref, *rest, sub, res_w, n_next):
    next_f32, o_ref, next_bf16, h_ref = rest[:n_next], rest[n_next], rest[n_next + 1:-1], rest[-1]
    j = pl.program_id(1)

    @pl.when(j == 0)
    def _():
        _modulated_prenorm(x_ref, gpre_ref, mod_ref, sub, h_ref, zero_ref=o_ref)

    h = h_ref[...]
    g = _dot(h, wg_ref[...])
    u = _dot(h, wu_ref[...])
    a = (g * jax.nn.sigmoid(g) * u).astype(BF16)
    n_chunks = o_ref.shape[1] // FFN_DOWN_COLS
    for c in range(n_chunks):
        cols = slice(c * FFN_DOWN_COLS, (c + 1) * FFN_DOWN_COLS)
        o_ref[:, cols] += _dot(a, wd_ref[:, cols])
        if c == 0:
            _cast_blocks(next_f32, next_bf16)

    @pl.when(j == pl.num_programs(1) - 1)
    def _():
        _gated_postnorm_residual(x_ref, o_ref, gpost_ref, mod_ref, sub, res_w, o_ref)


def _cast_rows(n_rows, n_steps):
    for r in range(16, n_rows + 1, 16):
        if n_rows % r == 0 and n_rows // r <= n_steps:
            return r
    raise ValueError((n_rows, n_steps))


def _add_cast_ahead(next_w, n_steps, step_of, in_specs, out_specs, out_shape, operands):
    if next_w is None:
        return 0
    l_next, mats = next_w
    for m in mats:
        _, rows, cols = m.shape
        r = _cast_rows(rows, n_steps)
        last = rows // r - 1
        in_specs.append(pl.BlockSpec(
            (None, r, cols), lambda *g, last=last: (l_next, jnp.minimum(step_of(*g), last), 0)))
        out_specs.append(pl.BlockSpec((r, cols), lambda *g, last=last: (jnp.minimum(step_of(*g), last), 0)))
        out_shape.append(jax.ShapeDtypeStruct((rows, cols), BF16))
        operands.append(m)
    return len(mats)


def _cast_blocks(srcs, dsts):
    for src, dst in zip(srcs, dsts):
        dst[...] = src[...].astype(BF16)


def _ffn_sublayer(x2, mods_l, gpre, gpost, w, next_w, *, sub, seq, tm=1024, tf=512):
    wg, wu, wd = w
    M, D = x2.shape
    F = wg.shape[-1]
    tpb = seq // tm
    n_i, n_j = M // tm, F // tf
    in_specs = [
        pl.BlockSpec((tm, D), lambda i, j: (i, 0)),
        pl.BlockSpec((None, N_SUB * N_MOD, D), lambda i, j: (i // tpb, 0, 0)),
        pl.BlockSpec((1, D), lambda i, j: (0, 0)),
        pl.BlockSpec((1, D), lambda i, j: (0, 0)),
        pl.BlockSpec((D, tf), lambda i, j: (0, j)),
        pl.BlockSpec((D, tf), lambda i, j: (0, j)),
        pl.BlockSpec((tf, D), lambda i, j: (j, 0)),
    ]
    out_specs = [pl.BlockSpec((tm, D), lambda i, j: (i, 0))]
    out_shape = [jax.ShapeDtypeStruct((M, D), F32)]
    operands = [x2, mods_l, gpre, gpost, wg, wu, wd]
    n_next = _add_cast_ahead(next_w, n_i * n_j, lambda i, j: i * n_j + j, in_specs, out_specs, out_shape, operands)
    outs = pl.pallas_call(
        functools.partial(_ffn_kernel, sub=sub, res_w=0.5, n_next=n_next),
        grid=(n_i, n_j),
        in_specs=in_specs,
        out_specs=out_specs,
        out_shape=out_shape,
        scratch_shapes=[pltpu.VMEM((tm, D), BF16)],
        compiler_params=_cparams(("arbitrary", "arbitrary")),
    )(*operands)
    return outs[0], tuple(outs[1:])


def _inproj_kernel(x_ref, mod_ref, gpre_ref, w_ref, *rest, heads_per_tile, tiles_per_section, n_next):
    next_f32, o_ref, next_bf16, h_ref = rest[:n_next], rest[n_next], rest[n_next + 1:-1], rest[-1]
    n = pl.program_id(1)

    @pl.when(n == 0)
    def _():
        _modulated_prenorm(x_ref, gpre_ref, mod_ref, 1, h_ref)

    res = _dot(h_ref[...], w_ref[...])
    res = res * jnp.where((n // tiles_per_section) % 3 == 0, QK_SCALE, 1.0)
    _cast_blocks(next_f32, next_bf16)
    for hh in range(heads_per_tile):
        o_ref[hh] = res[:, hh * HEAD_DIM:(hh + 1) * HEAD_DIM].astype(BF16)


def _inproj(x2, mods_l, gpre, w_in, next_w, *, batch, seq, tm=1024, tn=1024):
    M, D = x2.shape
    N = w_in.shape[-1]
    section = N // 6
    tn = min(tn, section)
    assert section % tn == 0
    tpb = seq // tm
    hpt = tn // HEAD_DIM
    n_i, n_n = M // tm, N // tn
    in_specs = [
        pl.BlockSpec((tm, D), lambda i, n: (i, 0)),
        pl.BlockSpec((None, N_SUB * N_MOD, D), lambda i, n: (i // tpb, 0, 0)),
        pl.BlockSpec((1, D), lambda i, n: (0, 0)),
        pl.BlockSpec((D, tn), lambda i, n: (0, n)),
    ]
    out_specs = [pl.BlockSpec((None, hpt, tm, HEAD_DIM), lambda i, n: (i // tpb, n, i % tpb, 0))]
    out_shape = [jax.ShapeDtypeStruct((batch, N // HEAD_DIM, seq, HEAD_DIM), BF16)]
    operands = [x2, mods_l, gpre, w_in]
    n_next = _add_cast_ahead(next_w, n_i * n_n, lambda i, n: i * n_n + n, in_specs, out_specs, out_shape, operands)
    outs = pl.pallas_call(
        functools.partial(_inproj_kernel, heads_per_tile=hpt, tiles_per_section=section // tn, n_next=n_next),
        grid=(n_i, n_n),
        in_specs=in_specs,
        out_specs=out_specs,
        out_shape=out_shape,
        scratch_shapes=[pltpu.VMEM((tm, D), BF16)],
        compiler_params=_cparams(("arbitrary", "arbitrary")),
    )(*operands)
    return outs[0], tuple(outs[1:])


def _t5_bucket_np(dist):
    n = np.maximum(dist, 0)
    max_exact = REL_BUCKETS // 2
    nf = np.maximum(n, 1).astype(np.float32)
    large = max_exact + (np.log(nf / np.float32(max_exact)) / np.float32(math.log(REL_MAX_DIST / max_exact))
                         * np.float32(REL_BUCKETS - max_exact)).astype(np.int32)
    large = np.minimum(large, REL_BUCKETS - 1)
    return np.where(n < max_exact, n, large).astype(np.int32)


def _bias_table_kernel(rb_ref, bucket_ref, o_ref):
    h = pl.program_id(0)
    for t in range(2):
        bucket = bucket_ref[t]
        tab = jnp.zeros(bucket.shape, F32)
        for b in range(REL_BUCKETS):
            tab = jnp.where(bucket == b, rb_ref[h, b], tab)
        if t == 0:
            row = lax.broadcasted_iota(jnp.int32, bucket.shape, 0)
            col = lax.broadcasted_iota(jnp.int32, bucket.shape, 1)
            tab = jnp.where(row >= col, tab, -jnp.inf)
        o_ref[t] = tab * LOG2E


def _bias_tables(rel_bias_t):
    H = rel_bias_t.shape[0]
    r = np.arange(MOBA_BLOCK)
    rel = r[:, None] - r[None, :]
    buckets = jnp.asarray(np.stack([_t5_bucket_np(rel), _t5_bucket_np(rel + MOBA_BLOCK)]))
    return pl.pallas_call(
        _bias_table_kernel,
        grid=(H,),
        in_specs=[
            pl.BlockSpec(memory_space=pltpu.SMEM),
            pl.BlockSpec((2, MOBA_BLOCK, MOBA_BLOCK), lambda h: (0, 0, 0)),
        ],
        out_specs=pl.BlockSpec((None, 2, MOBA_BLOCK, MOBA_BLOCK), lambda h: (h, 0, 0, 0)),
        out_shape=jax.ShapeDtypeStruct((H, 2, MOBA_BLOCK, MOBA_BLOCK), F32),
        compiler_params=_cparams(("arbitrary",)),
    )(rel_bias_t, buckets)


def _head_out(acc, gain):
    return (_rms(acc) * gain).astype(BF16)


def _moba_stages(h, rb_ref, q_ref, k_ref, v_ref, tab_ref, gain_ref, o_ref, s_ref, p_ref, v1_ref, *, nb):
    BLK = MOBA_BLOCK
    HALF = BLK // 2
    GATE_ROWS = 16
    assert nb <= GATE_ROWS
    b_far = rb_ref[h, REL_BUCKETS - 1] * LOG2E

    v1_ref[:, :HEAD_DIM] = v_ref[...]
    v1_ref[:, HEAD_DIM:] = jnp.ones((nb * BLK, HEAD_DIM), BF16)

    kf = k_ref[...].astype(F32).reshape(nb, BLK, HEAD_DIM)
    k_mean = jnp.mean(kf, axis=1)
    k_mean = jnp.concatenate([k_mean, jnp.zeros((GATE_ROWS - nb, HEAD_DIM), F32)], axis=0).astype(BF16)
    blk_id = lax.broadcasted_iota(jnp.int32, (GATE_ROWS, BLK), 0)

    def scores(i):
        qi = q_ref[i * BLK:(i + 1) * BLK, :]
        s_buf = s_ref.at[i % 2]
        if i > 0:
            gate = jnp.where(blk_id < i, _dot_nt(k_mean, qi), -jnp.inf)
            rank = jnp.zeros((GATE_ROWS, BLK), jnp.int32)
            for jp in range(i):
                cj = gate[jp:jp + 1, :]
                beats = (cj > gate) | ((cj == gate) & (jp < blk_id))
                rank = rank + beats.astype(jnp.int32)
            sel_t = jnp.where((rank < MOBA_TOPK) & (blk_id < i), 0.0, -jnp.inf)
            selw = jnp.concatenate([sel_t, jnp.zeros((HEAD_DIM - GATE_ROWS, BLK), F32)], axis=0).T

        m_run = None
        for j in range(i + 1):
            s = _dot_nt(qi, k_ref[j * BLK:(j + 1) * BLK, :])
            if j == i:
                s = s + tab_ref[0]
            elif j == i - 1:
                s = s + tab_ref[1] + selw[:, j:j + 1]
            else:
                s = s + (selw[:, j:j + 1] + b_far)
            s_buf[:, j * BLK:(j + 1) * BLK] = s
            mj = jnp.maximum(s[:, :HALF], s[:, HALF:])
            m_run = mj if m_run is None else jnp.maximum(m_run, mj)
        return jnp.max(m_run, axis=-1, keepdims=True)

    def output(i, m):
        s_buf = s_ref.at[i % 2]
        p_buf = p_ref.at[i % 2]
        for j in range(i + 1):
            p_buf[:, j * BLK:(j + 1) * BLK] = jnp.exp2(s_buf[:, j * BLK:(j + 1) * BLK] - m).astype(BF16)
        acc = _dot(p_buf[:, :(i + 1) * BLK], v1_ref[:(i + 1) * BLK, :])
        o_ref[i * BLK:(i + 1) * BLK, :] = _head_out(acc[:, :HEAD_DIM] / acc[:, HEAD_DIM:HEAD_DIM + 1],
                                                    gain_ref[...])

    return scores, output


def _sb_stages(q_ref, k_ref, v_ref, gain_ref, o_ref, acc_ref, rem_ref, *, nb):
    BLK = SB_BLOCK
    row = lax.broadcasted_iota(jnp.int32, (BLK, BLK), 0)
    col = lax.broadcasted_iota(jnp.int32, (BLK, BLK), 1)
    strict = col < row
    after = jnp.where(row > col, 1.0, 0.0).astype(BF16)

    def tail_sums(x):
        return _dot(x.astype(BF16), after)

    def strip_logits(qi, k_strip, diagonal):
        n = k_strip.shape[0] // BLK
        z = _dot_nt(qi, k_strip)
        t = jnp.exp2(z)
        e = jnp.minimum(t, 1.0 / t)
        fail = jnp.maximum(z, 0.0) + jnp.log2(1.0 + e)
        parts = []
        for b in range(n):
            cols = slice(b * BLK, (b + 1) * BLK)
            f_b = fail[:, cols]
            f_m = jnp.where(strict, f_b, 0.0) if diagonal and b == n - 1 else f_b
            w = tail_sums(f_m)
            parts.append((z[:, cols] - f_b - w, w[:, 0:1] + f_m[:, 0:1]))
        return parts

    def strip_output(parts, v_strip, rem, diagonal):
        n = len(parts)
        weights = [None] * n
        for b in reversed(range(n)):
            log_a, spent = parts[b]
            a = jnp.exp2(log_a if rem is None else log_a + rem)
            if diagonal and b == n - 1:
                a = jnp.where(strict, a, 0.0)
            weights[b] = a.astype(BF16)
            rem = -spent if rem is None else rem - spent
        a_all = weights[0] if n == 1 else jnp.concatenate(weights, axis=1)
        return _dot(a_all, v_strip), rem

    def strip(qi, k_strip, v_strip, rem, diagonal):
        return strip_output(strip_logits(qi, k_strip, diagonal), v_strip, rem, diagonal)

    def alive(rem):
        return (jnp.max(rem) > SB_LOG_ZERO * LOG2E).astype(jnp.int32)

    def near_keys(i):
        return slice(max(0, i - (SB_NEAR_BLOCKS - 1)) * BLK, (i + 1) * BLK)

    pending = []

    def logits(i):
        return strip_logits(q_ref[i * BLK:(i + 1) * BLK, :], k_ref[near_keys(i), :], True)

    def output(i, parts):
        acc, rem = strip_output(parts, v_ref[near_keys(i), :], None, True)
        if near_keys(i).start == 0:
            o_ref[i * BLK:(i + 1) * BLK, :] = _head_out(acc, gain_ref[...])
        else:
            acc_ref[i] = acc
            rem_ref[i] = jnp.broadcast_to(rem, (BLK, HEAD_DIM))
            pending.append((i, near_keys(i).start // BLK - 1, alive(rem)))

    def older():
        for i, next_blk, alive0 in pending:
            qi = q_ref[i * BLK:(i + 1) * BLK, :]

            def trip(c, n, qi=qi):
                j, _, rem, acc = c
                keys = pl.ds(pl.multiple_of((j - (n - 1)) * BLK, BLK), n * BLK)
                add, rem = strip(qi, k_ref[keys, :], v_ref[keys, :], rem, False)
                return j - n, alive(rem), rem, acc + add

            state = (jnp.int32(next_blk), alive0, rem_ref[i][:, 0:1], acc_ref[i])
            if next_blk >= 1:
                state = lax.while_loop(lambda c: (c[0] >= 1) & (c[1] > 0), functools.partial(trip, n=2), state)
            state = lax.while_loop(lambda c: (c[0] >= 0) & (c[1] > 0), functools.partial(trip, n=1), state)
            o_ref[i * BLK:(i + 1) * BLK, :] = _head_out(state[3], gain_ref[...])

    return logits, output, older


def _attn_kernel(rb_ref, qa_ref, ka_ref, va_ref, qb_ref, kb_ref, vb_ref, tab_ref, gain_a_ref, gain_b_ref,
                 oa_ref, ob_ref, s_ref, p_ref, v1_ref, acc_ref, rem_ref, *, nb):
    g = pl.program_id(1)
    a_scores, a_output = _moba_stages(g, rb_ref, qa_ref, ka_ref, va_ref, tab_ref, gain_a_ref, oa_ref,
                                      s_ref, p_ref, v1_ref, nb=nb)
    b_logits, b_output, b_older = _sb_stages(qb_ref, kb_ref, vb_ref, gain_b_ref, ob_ref, acc_ref, rem_ref, nb=nb)
    m, parts = a_scores(0), b_logits(0)
    for i in range(nb):
        if i + 1 < nb:
            m_next, parts_next = a_scores(i + 1), b_logits(i + 1)
        a_output(i, m)
        b_output(i, parts)
        m, parts = m_next, parts_next
    b_older()


def _attention(proj, rel_bias_t, tables, gain2, *, n_heads):
    B, _, S, _ = proj.shape
    nb = S // MOBA_BLOCK
    blk = (None, None, S, HEAD_DIM)

    def col_head(which):
        return pl.BlockSpec(blk, lambda b, g: (b, which * n_heads + g, 0, 0))

    out_spec = pl.BlockSpec((None, S, HEAD_DIM), lambda b, g: (b, 0, g))
    out_shape = jax.ShapeDtypeStruct((B, S, n_heads * HEAD_DIM), BF16)
    return pl.pallas_call(
        functools.partial(_attn_kernel, nb=nb),
        grid=(B, n_heads),
        in_specs=[
            pl.BlockSpec(memory_space=pltpu.SMEM),
            *[col_head(which) for which in range(6)],
            pl.BlockSpec((None, 2, MOBA_BLOCK, MOBA_BLOCK), lambda b, g: (g, 0, 0, 0)),
            pl.BlockSpec((1, HEAD_DIM), lambda b, g: (0, g)),
            pl.BlockSpec((1, HEAD_DIM), lambda b, g: (0, n_heads + g)),
        ],
        out_specs=[out_spec, out_spec],
        out_shape=[out_shape, out_shape],
        scratch_shapes=[
            pltpu.VMEM((2, MOBA_BLOCK, S), F32),
            pltpu.VMEM((2, MOBA_BLOCK, S), BF16),
            pltpu.VMEM((S, 2 * HEAD_DIM), BF16),
            pltpu.VMEM((nb, SB_BLOCK, HEAD_DIM), F32),
            pltpu.VMEM((nb, SB_BLOCK, HEAD_DIM), F32),
        ],
        compiler_params=_cparams(("parallel", "parallel")),
    )(rel_bias_t, *([proj] * 6), tables, gain2, gain2)


def _outproj_kernel(x_ref, oa_ref, ob_ref, mod_ref, gpost_ref, wa_ref, wb_ref, *rest, n_next):
    next_f32, out_ref, next_bf16 = rest[:n_next], rest[n_next], rest[n_next + 1:]
    out_ref[...] = _dot(oa_ref[...], wa_ref[...]) + _dot(ob_ref[...], wb_ref[...])
    _cast_blocks(next_f32, next_bf16)
    _gated_postnorm_residual(x_ref, out_ref, gpost_ref, mod_ref, 1, 1.0, out_ref)


def _outproj(x2, oa2, ob2, mods_l, gpost, w_out, next_w, *, seq, tm=512):
    M, D = x2.shape
    K = oa2.shape[-1]
    tpb = seq // tm
    in_specs = [
        pl.BlockSpec((tm, D), lambda i: (i, 0)),
        pl.BlockSpec((tm, K), lambda i: (i, 0)),
        pl.BlockSpec((tm, K), lambda i: (i, 0)),
        pl.BlockSpec((None, N_SUB * N_MOD, D), lambda i: (i // tpb, 0, 0)),
        pl.BlockSpec((1, D), lambda i: (0, 0)),
        pl.BlockSpec((K, D), lambda i: (0, 0)),
        pl.BlockSpec((K, D), lambda i: (1, 0)),
    ]
    out_specs = [pl.BlockSpec((tm, D), lambda i: (i, 0))]
    out_shape = [jax.ShapeDtypeStruct((M, D), F32)]
    operands = [x2, oa2, ob2, mods_l, gpost, w_out, w_out]
    n_next = _add_cast_ahead(next_w, M // tm, lambda i: i, in_specs, out_specs, out_shape, operands)
    outs = pl.pallas_call(
        functools.partial(_outproj_kernel, n_next=n_next),
        grid=(M // tm,),
        in_specs=in_specs,
        out_specs=out_specs,
        out_shape=out_shape,
        compiler_params=_cparams(("arbitrary",)),
    )(*operands)
    return outs[0], tuple(outs[1:])


def kernel(x, c, ada_w, ada_b, norm_pre, norm_post, ffn1_w_gate, ffn1_w_up, ffn1_w_down,
           mix_w_in, mix_out_gain, mix_w_out, rel_bias, ffn2_w_gate, ffn2_w_up, ffn2_w_down):
    B, S, D = x.shape
    L = ada_w.shape[0]
    n_heads = rel_bias.shape[1]
    d_mix = mix_w_out.shape[1]
    d_grp = n_heads * HEAD_DIM
    assert S % MOBA_BLOCK == 0 and d_mix == 2 * d_grp and mix_w_in.shape[-1] == 3 * d_mix

    ffn1 = [ffn1_w_gate, ffn1_w_up, ffn1_w_down]
    ffn2 = [ffn2_w_gate, ffn2_w_up, ffn2_w_down]
    c_pad = jnp.pad(c, ((0, 8 - B), (0, 0)))
    mods, (*w1, w_in) = _ada_mods(c_pad, ada_w, ada_b, (0, ffn1 + [mix_w_in]))
    mods = mods[:, :B].reshape(L, B, N_SUB * N_MOD, D)
    rel_bias_t = rel_bias.T
    tables = _bias_tables(rel_bias_t)

    x2 = x.reshape(B * S, D)
    for l in range(L):
        gpre = norm_pre[l].reshape(N_SUB, 1, D)
        gpost = norm_post[l].reshape(N_SUB, 1, D)
        gain2 = mix_out_gain[l].reshape(1, d_mix)
        x2, w2 = _ffn_sublayer(x2, mods[l], gpre[0], gpost[0], w1, (l, ffn2), sub=0, seq=S)
        proj, (w_out,) = _inproj(x2, mods[l], gpre[1], w_in, (l, [mix_w_out]), batch=B, seq=S)
        o_a, o_b = _attention(proj, rel_bias_t, tables, gain2, n_heads=n_heads)
        x2, w_in = _outproj(x2, o_a.reshape(B * S, d_grp), o_b.reshape(B * S, d_grp), mods[l], gpost[1], w_out,
                            (l + 1, [mix_w_in]) if l + 1 < L else None, seq=S)
        w_in = w_in[0] if w_in else None
        x2, w1 = _ffn_sublayer(x2, mods[l], gpre[2], gpost[2], w2, (l + 1, ffn1) if l + 1 < L else None,
                               sub=2, seq=S)
    return x2.reshape(B, S, D)
```

```python
import functools
import math

import numpy as np
import jax
import jax.numpy as jnp
from jax import lax
from jax.experimental import pallas as pl
from jax.experimental.pallas import tpu as pltpu

F32 = jnp.float32
BF16 = jnp.bfloat16

HEAD_DIM = 128
MOBA_BLOCK = 256
MOBA_TOPK = 3
REL_BUCKETS = 32
REL_MAX_DIST = 128
N_SUB = 3
N_MOD = 3
EPS = 1e-6
SB_BLOCK = 256
SB_LOG_ZERO = -104.0
SB_NEAR_BLOCKS = 3
LOG2E = math.log2(math.e)
FFN_DOWN_COLS = 512
QK_SCALE = HEAD_DIM ** -0.5 * LOG2E
SUBLANES = 8
ROW_CHUNK = 2 * SUBLANES
ROW_UNROLL = 8

V7X_VMEM_BYTES = 64 * 1024 * 1024
VMEM_LIMIT = V7X_VMEM_BYTES - 2 * 1024 * 1024


def _cparams(sem):
    return pltpu.CompilerParams(dimension_semantics=sem, vmem_limit_bytes=VMEM_LIMIT)


def _dot(a, b):
    return jnp.dot(a, b, preferred_element_type=F32)


def _dot_nt(a, b):
    return lax.dot_general(a, b, (((1,), (1,)), ((), ())), preferred_element_type=F32)


def _rms(x):
    return x * lax.rsqrt(jnp.mean(x * x, axis=-1, keepdims=True) + EPS)


def _row_chunks(n_rows, body):
    def step(r, carry):
        body(pl.ds(pl.multiple_of(r * ROW_CHUNK, ROW_CHUNK), ROW_CHUNK))
        return carry

    lax.fori_loop(0, n_rows // ROW_CHUNK, step, 0, unroll=ROW_UNROLL)


def _modulated_prenorm(x_ref, gpre_ref, mod_ref, sub, h_ref, zero_ref=None):
    gain = gpre_ref[...] * (1.0 + mod_ref[3 * sub + 1:3 * sub + 2, :])
    shift = mod_ref[3 * sub:3 * sub + 1, :]

    def body(rows):
        h_ref[rows, :] = (_rms(x_ref[rows, :]) * gain + shift).astype(BF16)
        if zero_ref is not None:
            zero_ref[rows, :] = jnp.zeros((ROW_CHUNK, zero_ref.shape[1]), zero_ref.dtype)

    _row_chunks(x_ref.shape[0], body)


def _gated_postnorm_residual(x_ref, y_ref, gpost_ref, mod_ref, sub, res_w, o_ref):
    coef = (res_w * mod_ref[3 * sub + 2:3 * sub + 3, :]) * gpost_ref[...]

    for r in range(0, x_ref.shape[0], SUBLANES):
        rows = slice(r, r + SUBLANES)
        o_ref[rows, :] = x_ref[rows, :] + _rms(y_ref[rows, :]) * coef


def _ada_kernel(c_ref, w_ref, b_ref, *rest, n_next):
    next_f32, o_ref, next_bf16 = rest[:n_next], rest[n_next], rest[n_next + 1:]
    c = c_ref[...]
    ca = (c * jax.nn.sigmoid(c)).astype(BF16)
    o_ref[...] = _dot(ca, w_ref[...].astype(BF16)) + b_ref[...]
    _cast_blocks(next_f32, next_bf16)


def _ada_mods(c_pad, ada_w, ada_b, next_w):
    L, D, N = ada_w.shape
    tn = min(1024, D)
    n_n = N // tn
    in_specs = [
        pl.BlockSpec((SUBLANES, D), lambda l, n: (0, 0)),
        pl.BlockSpec((None, D, tn), lambda l, n: (l, 0, n)),
        pl.BlockSpec((None, 1, tn), lambda l, n: (l, 0, n)),
    ]
    out_specs = [pl.BlockSpec((None, SUBLANES, tn), lambda l, n: (l, 0, n))]
    out_shape = [jax.ShapeDtypeStruct((L, SUBLANES, N), F32)]
    operands = [c_pad, ada_w, ada_b.reshape(L, 1, N)]
    n_next = _add_cast_ahead(next_w, L * n_n, lambda l, n: l * n_n + n, in_specs, out_specs, out_shape, operands)
    outs = pl.pallas_call(
        functools.partial(_ada_kernel, n_next=n_next),
        grid=(L, n_n),
        in_specs=in_specs,
        out_specs=out_specs,
        out_shape=out_shape,
        compiler_params=_cparams(("arbitrary", "arbitrary")),
    )(*operands)
    return outs[0], tuple(outs[1:])


def _ffn_kernel(x_ref, mod_ref, gpre_ref, gpost_ref, wg_ref, wu_ref, wd_ref, *rest, sub, res_w, n_next):
    next_f32, o_ref, next_bf16, h_ref = rest[:n_next], rest[n_next], rest[n_next + 1:-1], rest[-1]
    j = pl.program_id(1)

    @pl.when(j == 0)
    def _():
        _modulated_prenorm(x_ref, gpre_ref, mod_ref, sub, h_ref, zero_ref=o_ref)

    h = h_ref[...]
    g = _dot(h, wg_ref[...])
    u = _dot(h, wu_ref[...])
    a = (g * jax.nn.sigmoid(g) * u).astype(BF16)
    n_chunks = o_ref.shape[1] // FFN_DOWN_COLS
    for c in range(n_chunks):
        cols = slice(c * FFN_DOWN_COLS, (c + 1) * FFN_DOWN_COLS)
        o_ref[:, cols] += _dot(a, wd_ref[:, cols])
        if c == 0:
            _cast_blocks(next_f32, next_bf16)

    @pl.when(j == pl.num_programs(1) - 1)
    def _():
        _gated_postnorm_residual(x_ref, o_ref, gpost_ref, mod_ref, sub, res_w, o_ref)


def _cast_rows(n_rows, n_steps):
    for r in range(16, n_rows + 1, 16):
        if n_rows % r == 0 and n_rows // r <= n_steps:
            return r
    raise ValueError((n_rows, n_steps))


def _add_cast_ahead(next_w, n_steps, step_of, in_specs, out_specs, out_shape, operands):
    if next_w is None:
        return 0
    l_next, mats = next_w
    for m in mats:
        _, rows, cols = m.shape
        r = _cast_rows(rows, n_steps)
        last = rows // r - 1
        in_specs.append(pl.BlockSpec(
            (None, r, cols), lambda *g, last=last: (l_next, jnp.minimum(step_of(*g), last), 0)))
        out_specs.append(pl.BlockSpec((r, cols), lambda *g, last=last: (jnp.minimum(step_of(*g), last), 0)))
        out_shape.append(jax.ShapeDtypeStruct((rows, cols), BF16))
        operands.append(m)
    return len(mats)


def _cast_blocks(srcs, dsts):
    for src, dst in zip(srcs, dsts):
        dst[...] = src[...].astype(BF16)


def _ffn_sublayer(x2, mods_l, gpre, gpost, w, next_w, *, sub, seq, tm=1024, tf=512):
    wg, wu, wd = w
    M, D = x2.shape
    F = wg.shape[-1]
    tpb = seq // tm
    n_i, n_j = M // tm, F // tf
    in_specs = [
        pl.BlockSpec((tm, D), lambda i, j: (i, 0)),
        pl.BlockSpec((None, N_SUB * N_MOD, D), lambda i, j: (i // tpb, 0, 0)),
        pl.BlockSpec((1, D), lambda i, j: (0, 0)),
        pl.BlockSpec((1, D), lambda i, j: (0, 0)),
        pl.BlockSpec((D, tf), lambda i, j: (0, j)),
        pl.BlockSpec((D, tf), lambda i, j: (0, j)),
        pl.BlockSpec((tf, D), lambda i, j: (j, 0)),
    ]
    out_specs = [pl.BlockSpec((tm, D), lambda i, j: (i, 0))]
    out_shape = [jax.ShapeDtypeStruct((M, D), F32)]
    operands = [x2, mods_l, gpre, gpost, wg, wu, wd]
    n_next = _add_cast_ahead(next_w, n_i * n_j, lambda i, j: i * n_j + j, in_specs, out_specs, out_shape, operands)
    outs = pl.pallas_call(
        functools.partial(_ffn_kernel, sub=sub, res_w=0.5, n_next=n_next),
        grid=(n_i, n_j),
        in_specs=in_specs,
        out_specs=out_specs,
        out_shape=out_shape,
        scratch_shapes=[pltpu.VMEM((tm, D), BF16)],
        compiler_params=_cparams(("arbitrary", "arbitrary")),
    )(*operands)
    return outs[0], tuple(outs[1:])


def _inproj_kernel(x_ref, mod_ref, gpre_ref, w_ref, *rest, heads_per_tile, tiles_per_section, n_next):
    next_f32, o_ref, next_bf16, h_ref = rest[:n_next], rest[n_next], rest[n_next + 1:-1], rest[-1]
    n = pl.program_id(1)

    @pl.when(n == 0)
    def _():
        _modulated_prenorm(x_ref, gpre_ref, mod_ref, 1, h_ref)

    res = _dot(h_ref[...], w_ref[...])
    res = res * jnp.where((n // tiles_per_section) % 3 == 0, QK_SCALE, 1.0)
    _cast_blocks(next_f32, next_bf16)
    for hh in range(heads_per_tile):
        o_ref[hh] = res[:, hh * HEAD_DIM:(hh + 1) * HEAD_DIM].astype(BF16)


def _inproj(x2, mods_l, gpre, w_in, next_w, *, batch, seq, tm=1024, tn=1024):
    M, D = x2.shape
    N = w_in.shape[-1]
    section = N // 6
    tn = min(tn, section)
    assert section % tn == 0
    tpb = seq // tm
    hpt = tn // HEAD_DIM
    n_i, n_n = M // tm, N // tn
    in_specs = [
        pl.BlockSpec((tm, D), lambda i, n: (i, 0)),
        pl.BlockSpec((None, N_SUB * N_MOD, D), lambda i, n: (i // tpb, 0, 0)),
        pl.BlockSpec((1, D), lambda i, n: (0, 0)),
        pl.BlockSpec((D, tn), lambda i, n: (0, n)),
    ]
    out_specs = [pl.BlockSpec((None, hpt, tm, HEAD_DIM), lambda i, n: (i // tpb, n, i % tpb, 0))]
    out_shape = [jax.ShapeDtypeStruct((batch, N // HEAD_DIM, seq, HEAD_DIM), BF16)]
    operands = [x2, mods_l, gpre, w_in]
    n_next = _add_cast_ahead(next_w, n_i * n_n, lambda i, n: i * n_n + n, in_specs, out_specs, out_shape, operands)
    outs = pl.pallas_call(
        functools.partial(_inproj_kernel, heads_per_tile=hpt, tiles_per_section=section // tn, n_next=n_next),
        grid=(n_i, n_n),
        in_specs=in_specs,
        out_specs=out_specs,
        out_shape=out_shape,
        scratch_shapes=[pltpu.VMEM((tm, D), BF16)],
        compiler_params=_cparams(("arbitrary", "arbitrary")),
    )(*operands)
    return outs[0], tuple(outs[1:])


def _t5_bucket_np(dist):
    n = np.maximum(dist, 0)
    max_exact = REL_BUCKETS // 2
    nf = np.maximum(n, 1).astype(np.float32)
    large = max_exact + (np.log(nf / np.float32(max_exact)) / np.float32(math.log(REL_MAX_DIST / max_exact))
                         * np.float32(REL_BUCKETS - max_exact)).astype(np.int32)
    large = np.minimum(large, REL_BUCKETS - 1)
    return np.where(n < max_exact, n, large).astype(np.int32)


def _bias_table_kernel(rb_ref, bucket_ref, o_ref):
    h = pl.program_id(0)
    for t in range(2):
        bucket = bucket_ref[t]
        tab = jnp.zeros(bucket.shape, F32)
        for b in range(REL_BUCKETS):
            tab = jnp.where(bucket == b, rb_ref[h, b], tab)
        if t == 0:
            row = lax.broadcasted_iota(jnp.int32, bucket.shape, 0)
            col = lax.broadcasted_iota(jnp.int32, bucket.shape, 1)
            tab = jnp.where(row >= col, tab, -jnp.inf)
        o_ref[t] = tab * LOG2E


def _bias_tables(rel_bias_t):
    H = rel_bias_t.shape[0]
    r = np.arange(MOBA_BLOCK)
    rel = r[:, None] - r[None, :]
    buckets = jnp.asarray(np.stack([_t5_bucket_np(rel), _t5_bucket_np(rel + MOBA_BLOCK)]))
    return pl.pallas_call(
        _bias_table_kernel,
        grid=(H,),
        in_specs=[
            pl.BlockSpec(memory_space=pltpu.SMEM),
            pl.BlockSpec((2, MOBA_BLOCK, MOBA_BLOCK), lambda h: (0, 0, 0)),
        ],
        out_specs=pl.BlockSpec((None, 2, MOBA_BLOCK, MOBA_BLOCK), lambda h: (h, 0, 0, 0)),
        out_shape=jax.ShapeDtypeStruct((H, 2, MOBA_BLOCK, MOBA_BLOCK), F32),
        compiler_params=_cparams(("arbitrary",)),
    )(rel_bias_t, buckets)


def _head_out(acc, gain):
    return (_rms(acc) * gain).astype(BF16)


def _moba_stages(h, rb_ref, q_ref, k_ref, v_ref, tab_ref, gain_ref, o_ref, s_ref, p_ref, v1_ref, *, nb):
    BLK = MOBA_BLOCK
    HALF = BLK // 2
    GATE_ROWS = 16
    assert nb <= GATE_ROWS
    b_far = rb_ref[h, REL_BUCKETS - 1] * LOG2E

    v1_ref[:, :HEAD_DIM] = v_ref[...]
    v1_ref[:, HEAD_DIM:] = jnp.ones((nb * BLK, HEAD_DIM), BF16)

    kf = k_ref[...].astype(F32).reshape(nb, BLK, HEAD_DIM)
    k_mean = jnp.mean(kf, axis=1)
    k_mean = jnp.concatenate([k_mean, jnp.zeros((GATE_ROWS - nb, HEAD_DIM), F32)], axis=0).astype(BF16)
    blk_id = lax.broadcasted_iota(jnp.int32, (GATE_ROWS, BLK), 0)

    def scores(i):
        qi = q_ref[i * BLK:(i + 1) * BLK, :]
        s_buf = s_ref.at[i % 2]
        if i > 0:
            gate = jnp.where(blk_id < i, _dot_nt(k_mean, qi), -jnp.inf)
            rank = jnp.zeros((GATE_ROWS, BLK), jnp.int32)
            for jp in range(i):
                cj = gate[jp:jp + 1, :]
                beats = (cj > gate) | ((cj == gate) & (jp < blk_id))
                rank = rank + beats.astype(jnp.int32)
            sel_t = jnp.where((rank < MOBA_TOPK) & (blk_id < i), 0.0, -jnp.inf)
            selw = jnp.concatenate([sel_t, jnp.zeros((HEAD_DIM - GATE_ROWS, BLK), F32)], axis=0).T

        m_run = None
        for j in range(i + 1):
            s = _dot_nt(qi, k_ref[j * BLK:(j + 1) * BLK, :])
            if j == i:
                s = s + tab_ref[0]
            elif j == i - 1:
                s = s + tab_ref[1] + selw[:, j:j + 1]
            else:
                s = s + (selw[:, j:j + 1] + b_far)
            s_buf[:, j * BLK:(j + 1) * BLK] = s
            mj = jnp.maximum(s[:, :HALF], s[:, HALF:])
            m_run = mj if m_run is None else jnp.maximum(m_run, mj)
        return jnp.max(m_run, axis=-1, keepdims=True)

    def output(i, m):
        s_buf = s_ref.at[i % 2]
        p_buf = p_ref.at[i % 2]
        for j in range(i + 1):
            p_buf[:, j * BLK:(j + 1) * BLK] = jnp.exp2(s_buf[:, j * BLK:(j + 1) * BLK] - m).astype(BF16)
        acc = _dot(p_buf[:, :(i + 1) * BLK], v1_ref[:(i + 1) * BLK, :])
        o_ref[i * BLK:(i + 1) * BLK, :] = _head_out(acc[:, :HEAD_DIM] / acc[:, HEAD_DIM:HEAD_DIM + 1],
                                                    gain_ref[...])

    return scores, output


def _sb_stages(q_ref, k_ref, v_ref, gain_ref, o_ref, acc_ref, rem_ref, *, nb):
    BLK = SB_BLOCK
    row = lax.broadcasted_iota(jnp.int32, (BLK, BLK), 0)
    col = lax.broadcasted_iota(jnp.int32, (BLK, BLK), 1)
    strict = col < row
    after = jnp.where(row > col, 1.0, 0.0).astype(BF16)

    def tail_sums(x):
        return _dot(x.astype(BF16), after)

    def strip_logits(qi, k_strip, diagonal):
        n = k_strip.shape[0] // BLK
        z = _dot_nt(qi, k_strip)
        t = jnp.exp2(z)
        e = jnp.minimum(t, 1.0 / t)
        fail = jnp.maximum(z, 0.0) + jnp.log2(1.0 + e)
        parts = []
        for b in range(n):
            cols = slice(b * BLK, (b + 1) * BLK)
            f_b = fail[:, cols]
            f_m = jnp.where(strict, f_b, 0.0) if diagonal and b == n - 1 else f_b
            w = tail_sums(f_m)
            parts.append((z[:, cols] - f_b - w, w[:, 0:1] + f_m[:, 0:1]))
        return parts

    def strip_output(parts, v_strip, rem, diagonal):
        n = len(parts)
        weights = [None] * n
        for b in reversed(range(n)):
            log_a, spent = parts[b]
            a = jnp.exp2(log_a if rem is None else log_a + rem)
            if diagonal and b == n - 1:
                a = jnp.where(strict, a, 0.0)
            weights[b] = a.astype(BF16)
            rem = -spent if rem is None else rem - spent
        a_all = weights[0] if n == 1 else jnp.concatenate(weights, axis=1)
        return _dot(a_all, v_strip), rem

    def strip(qi, k_strip, v_strip, rem, diagonal):
        return strip_output(strip_logits(qi, k_strip, diagonal), v_strip, rem, diagonal)

    def alive(rem):
        return (jnp.max(rem) > SB_LOG_ZERO * LOG2E).astype(jnp.int32)

    def near_keys(i):
        return slice(max(0, i - (SB_NEAR_BLOCKS - 1)) * BLK, (i + 1) * BLK)

    pending = []

    def logits(i):
        return strip_logits(q_ref[i * BLK:(i + 1) * BLK, :], k_ref[near_keys(i), :], True)

    def output(i, parts):
        acc, rem = strip_output(parts, v_ref[near_keys(i), :], None, True)
        if near_keys(i).start == 0:
            o_ref[i * BLK:(i + 1) * BLK, :] = _head_out(acc, gain_ref[...])
        else:
            acc_ref[i] = acc
            rem_ref[i] = jnp.broadcast_to(rem, (BLK, HEAD_DIM))
            pending.append((i, near_keys(i).start // BLK - 1, alive(rem)))

    def older():
        for i, next_blk, alive0 in pending:
            qi = q_ref[i * BLK:(i + 1) * BLK, :]

            def trip(c, n, qi=qi):
                j, _, rem, acc = c
                keys = pl.ds(pl.multiple_of((j - (n - 1)) * BLK, BLK), n * BLK)
                add, rem = strip(qi, k_ref[keys, :], v_ref[keys, :], rem, False)
                return j - n, alive(rem), rem, acc + add

            state = (jnp.int32(next_blk), alive0, rem_ref[i][:, 0:1], acc_ref[i])
            if next_blk >= 1:
                state = lax.while_loop(lambda c: (c[0] >= 1) & (c[1] > 0), functools.partial(trip, n=2), state)
            state = lax.while_loop(lambda c: (c[0] >= 0) & (c[1] > 0), functools.partial(trip, n=1), state)
            o_ref[i * BLK:(i + 1) * BLK, :] = _head_out(state[3], gain_ref[...])

    return logits, output, older


def _attn_kernel(rb_ref, qa_ref, ka_ref, va_ref, qb_ref, kb_ref, vb_ref, tab_ref, gain_a_ref, gain_b_ref,
                 oa_ref, ob_ref, s_ref, p_ref, v1_ref, acc_ref, rem_ref, *, nb):
    g = pl.program_id(1)
    a_scores, a_output = _moba_stages(g, rb_ref, qa_ref, ka_ref, va_ref, tab_ref, gain_a_ref, oa_ref,
                                      s_ref, p_ref, v1_ref, nb=nb)
    b_logits, b_output, b_older = _sb_stages(qb_ref, kb_ref, vb_ref, gain_b_ref, ob_ref, acc_ref, rem_ref, nb=nb)
    m, parts = a_scores(0), b_logits(0)
    for i in range(nb):
        if i + 1 < nb:
            m_next, parts_next = a_scores(i + 1), b_logits(i + 1)
        a_output(i, m)
        b_output(i, parts)
        m, parts = m_next, parts_next
    b_older()


def _attention(proj, rel_bias_t, tables, gain2, *, n_heads):
    B, _, S, _ = proj.shape
    nb = S // MOBA_BLOCK
    blk = (None, None, S, HEAD_DIM)

    def col_head(which):
        return pl.BlockSpec(blk, lambda b, g: (b, which * n_heads + g, 0, 0))

    out_spec = pl.BlockSpec((None, S, HEAD_DIM), lambda b, g: (b, 0, g))
    out_shape = jax.ShapeDtypeStruct((B, S, n_heads * HEAD_DIM), BF16)
    return pl.pallas_call(
        functools.partial(_attn_kernel, nb=nb),
        grid=(B, n_heads),
        in_specs=[
            pl.BlockSpec(memory_space=pltpu.SMEM),
            *[col_head(which) for which in range(6)],
            pl.BlockSpec((None, 2, MOBA_BLOCK, MOBA_BLOCK), lambda b, g: (g, 0, 0, 0)),
            pl.BlockSpec((1, HEAD_DIM), lambda b, g: (0, g)),
            pl.BlockSpec((1, HEAD_DIM), lambda b, g: (0, n_heads + g)),
        ],
        out_specs=[out_spec, out_spec],
        out_shape=[out_shape, out_shape],
        scratch_shapes=[
            pltpu.VMEM((2, MOBA_BLOCK, S), F32),
            pltpu.VMEM((2, MOBA_BLOCK, S), BF16),
            pltpu.VMEM((S, 2 * HEAD_DIM), BF16),
            pltpu.VMEM((nb, SB_BLOCK, HEAD_DIM), F32),
            pltpu.VMEM((nb, SB_BLOCK, HEAD_DIM), F32),
        ],
        compiler_params=_cparams(("parallel", "parallel")),
    )(rel_bias_t, *([proj] * 6), tables, gain2, gain2)


def _outproj_kernel(x_ref, oa_ref, ob_ref, mod_ref, gpost_ref, wa_ref, wb_ref, *rest, n_next):
    next_f32, out_ref, next_bf16 = rest[:n_next], rest[n_next], rest[n_next + 1:]
    out_ref[...] = _dot(oa_ref[...], wa_ref[...]) + _dot(ob_ref[...], wb_ref[...])
    _cast_blocks(next_f32, next_bf16)
    _gated_postnorm_residual(x_ref, out_ref, gpost_ref, mod_ref, 1, 1.0, out_ref)


def _outproj(x2, oa2, ob2, mods_l, gpost, w_out, next_w, *, seq, tm=512):
    M, D = x2.shape
    K = oa2.shape[-1]
    tpb = seq // tm
    in_specs = [
        pl.BlockSpec((tm, D), lambda i: (i, 0)),
        pl.BlockSpec((tm, K), lambda i: (i, 0)),
        pl.BlockSpec((tm, K), lambda i: (i, 0)),
        pl.BlockSpec((None, N_SUB * N_MOD, D), lambda i: (i // tpb, 0, 0)),
        pl.BlockSpec((1, D), lambda i: (0, 0)),
        pl.BlockSpec((K, D), lambda i: (0, 0)),
        pl.BlockSpec((K, D), lambda i: (1, 0)),
    ]
    out_specs = [pl.BlockSpec((tm, D), lambda i: (i, 0))]
    out_shape = [jax.ShapeDtypeStruct((M, D), F32)]
    operands = [x2, oa2, ob2, mods_l, gpost, w_out, w_out]
    n_next = _add_cast_ahead(next_w, M // tm, lambda i: i, in_specs, out_specs, out_shape, operands)
    outs = pl.pallas_call(
        functools.partial(_outproj_kernel, n_next=n_next),
        grid=(M // tm,),
        in_specs=in_specs,
        out_specs=out_specs,
        out_shape=out_shape,
        compiler_params=_cparams(("arbitrary",)),
    )(*operands)
    return outs[0], tuple(outs[1:])


def kernel(x, c, ada_w, ada_b, norm_pre, norm_post, ffn1_w_gate, ffn1_w_up, ffn1_w_down,
           mix_w_in, mix_out_gain, mix_w_out, rel_bias, ffn2_w_gate, ffn2_w_up, ffn2_w_down):
    B, S, D = x.shape
    L = ada_w.shape[0]
    n_heads = rel_bias.shape[1]
    d_mix = mix_w_out.shape[1]
    d_grp = n_heads * HEAD_DIM
    assert S % MOBA_BLOCK == 0 and d_mix == 2 * d_grp and mix_w_in.shape[-1] == 3 * d_mix

    ffn1 = [ffn1_w_gate, ffn1_w_up, ffn1_w_down]
    ffn2 = [ffn2_w_gate, ffn2_w_up, ffn2_w_down]
    assert B <= SUBLANES
    c_pad = jnp.pad(c, ((0, SUBLANES - B), (0, 0)))
    mods, (*w1, w_in) = _ada_mods(c_pad, ada_w, ada_b, (0, ffn1 + [mix_w_in]))
    mods = mods[:, :B].reshape(L, B, N_SUB * N_MOD, D)
    rel_bias_t = rel_bias.T
    tables = _bias_tables(rel_bias_t)

    x2 = x.reshape(B * S, D)
    for l in range(L):
        gpre = norm_pre[l].reshape(N_SUB, 1, D)
        gpost = norm_post[l].reshape(N_SUB, 1, D)
        gain2 = mix_out_gain[l].reshape(1, d_mix)
        x2, w2 = _ffn_sublayer(x2, mods[l], gpre[0], gpost[0], w1, (l, ffn2), sub=0, seq=S)
        proj, (w_out,) = _inproj(x2, mods[l], gpre[1], w_in, (l, [mix_w_out]), batch=B, seq=S)
        o_a, o_b = _attention(proj, rel_bias_t, tables, gain2, n_heads=n_heads)
        x2, w_in = _outproj(x2, o_a.reshape(B * S, d_grp), o_b.reshape(B * S, d_grp), mods[l], gpost[1], w_out,
                            (l + 1, [mix_w_in]) if l + 1 < L else None, seq=S)
        w_in = w_in[0] if w_in else None
        x2, w1 = _ffn_sublayer(x2, mods[l], gpre[2], gpost[2], w2, (l + 1, ffn1) if l + 1 < L else None,
                               sub=2, seq=S)
    return x2.reshape(B, S, D)
```

```python
import functools
import math

import numpy as np
import jax
import jax.numpy as jnp
from jax import lax
from jax.experimental import pallas as pl
from jax.experimental.pallas import tpu as pltpu

F32 = jnp.float32
BF16 = jnp.bfloat16

HEAD_DIM = 128
MOBA_BLOCK = 256
MOBA_TOPK = 3
REL_BUCKETS = 32
REL_MAX_DIST = 128
N_SUB = 3
N_MOD = 3
EPS = 1e-6
SB_BLOCK = 256
SB_LOG_ZERO = -104.0
SB_NEAR_BLOCKS = 4
LOG2E = math.log2(math.e)
FFN_DOWN_COLS = 512
QK_SCALE = HEAD_DIM ** -0.5 * LOG2E
SUBLANES = 8
ROW_CHUNK = 2 * SUBLANES
ROW_UNROLL = 8

V7X_VMEM_BYTES = 64 * 1024 * 1024
VMEM_LIMIT = V7X_VMEM_BYTES - 2 * 1024 * 1024


def _cparams(sem):
    return pltpu.CompilerParams(dimension_semantics=sem, vmem_limit_bytes=VMEM_LIMIT)


def _dot(a, b):
    return jnp.dot(a, b, preferred_element_type=F32)


def _dot_nt(a, b):
    return lax.dot_general(a, b, (((1,), (1,)), ((), ())), preferred_element_type=F32)


def _rms(x):
    return x * lax.rsqrt(jnp.mean(x * x, axis=-1, keepdims=True) + EPS)


def _row_chunks(n_rows, body):
    def step(r, carry):
        body(pl.ds(pl.multiple_of(r * ROW_CHUNK, ROW_CHUNK), ROW_CHUNK))
        return carry

    lax.fori_loop(0, n_rows // ROW_CHUNK, step, 0, unroll=ROW_UNROLL)


def _modulated_prenorm(x_ref, gpre_ref, mod_ref, sub, h_ref, zero_ref=None):
    gain = gpre_ref[...] * (1.0 + mod_ref[3 * sub + 1:3 * sub + 2, :])
    shift = mod_ref[3 * sub:3 * sub + 1, :]

    def body(rows):
        h_ref[rows, :] = (_rms(x_ref[rows, :]) * gain + shift).astype(BF16)
        if zero_ref is not None:
            zero_ref[rows, :] = jnp.zeros((ROW_CHUNK, zero_ref.shape[1]), zero_ref.dtype)

    _row_chunks(x_ref.shape[0], body)


def _gated_postnorm_residual(x_ref, y_ref, gpost_ref, mod_ref, sub, res_w, o_ref):
    coef = (res_w * mod_ref[3 * sub + 2:3 * sub + 3, :]) * gpost_ref[...]

    for r in range(0, x_ref.shape[0], SUBLANES):
        rows = slice(r, r + SUBLANES)
        o_ref[rows, :] = x_ref[rows, :] + _rms(y_ref[rows, :]) * coef


def _ada_kernel(c_ref, w_ref, b_ref, *rest, n_next):
    next_f32, o_ref, next_bf16 = rest[:n_next], rest[n_next], rest[n_next + 1:]
    c = c_ref[...]
    ca = (c * jax.nn.sigmoid(c)).astype(BF16)
    o_ref[...] = _dot(ca, w_ref[...].astype(BF16)) + b_ref[...]
    _cast_blocks(next_f32, next_bf16)


def _ada_mods(c_pad, ada_w, ada_b, next_w):
    L, D, N = ada_w.shape
    tn = min(1024, D)
    n_n = N // tn
    in_specs = [
        pl.BlockSpec((SUBLANES, D), lambda l, n: (0, 0)),
        pl.BlockSpec((None, D, tn), lambda l, n: (l, 0, n)),
        pl.BlockSpec((None, 1, tn), lambda l, n: (l, 0, n)),
    ]
    out_specs = [pl.BlockSpec((None, SUBLANES, tn), lambda l, n: (l, 0, n))]
    out_shape = [jax.ShapeDtypeStruct((L, SUBLANES, N), F32)]
    operands = [c_pad, ada_w, ada_b.reshape(L, 1, N)]
    n_next = _add_cast_ahead(next_w, L * n_n, lambda l, n: l * n_n + n, in_specs, out_specs, out_shape, operands)
    outs = pl.pallas_call(
        functools.partial(_ada_kernel, n_next=n_next),
        grid=(L, n_n),
        in_specs=in_specs,
        out_specs=out_specs,
        out_shape=out_shape,
        compiler_params=_cparams(("arbitrary", "arbitrary")),
    )(*operands)
    return outs[0], tuple(outs[1:])


def _ffn_kernel(x_ref, mod_ref, gpre_ref, gpost_ref, wg_ref, wu_ref, wd_ref, *rest, sub, res_w, n_next):
    next_f32, o_ref, next_bf16, h_ref = rest[:n_next], rest[n_next], rest[n_next + 1:-1], rest[-1]
    j = pl.program_id(1)

    @pl.when(j == 0)
    def _():
        _modulated_prenorm(x_ref, gpre_ref, mod_ref, sub, h_ref, zero_ref=o_ref)

    h = h_ref[...]
    g = _dot(h, wg_ref[...])
    u = _dot(h, wu_ref[...])
    a = (g * jax.nn.sigmoid(g) * u).astype(BF16)
    n_chunks = o_ref.shape[1] // FFN_DOWN_COLS
    for c in range(n_chunks):
        cols = slice(c * FFN_DOWN_COLS, (c + 1) * FFN_DOWN_COLS)
        o_ref[:, cols] += _dot(a, wd_ref[:, cols])
        if c == 0:
            _cast_blocks(next_f32, next_bf16)

    @pl.when(j == pl.num_programs(1) - 1)
    def _():
        _gated_postnorm_residual(x_ref, o_ref, gpost_ref, mod_ref, sub, res_w, o_ref)


def _cast_rows(n_rows, n_steps):
    for r in range(16, n_rows + 1, 16):
        if n_rows % r == 0 and n_rows // r <= n_steps:
            return r
    raise ValueError((n_rows, n_steps))


def _add_cast_ahead(next_w, n_steps, step_of, in_specs, out_specs, out_shape, operands):
    if next_w is None:
        return 0
    l_next, mats = next_w
    for m in mats:
        _, rows, cols = m.shape
        r = _cast_rows(rows, n_steps)
        last = rows // r - 1
        in_specs.append(pl.BlockSpec(
            (None, r, cols), lambda *g, last=last: (l_next, jnp.minimum(step_of(*g), last), 0)))
        out_specs.append(pl.BlockSpec((r, cols), lambda *g, last=last: (jnp.minimum(step_of(*g), last), 0)))
        out_shape.append(jax.ShapeDtypeStruct((rows, cols), BF16))
        operands.append(m)
    return len(mats)


def _cast_blocks(srcs, dsts):
    for src, dst in zip(srcs, dsts):
        dst[...] = src[...].astype(BF16)


def _ffn_sublayer(x2, mods_l, gpre, gpost, w, next_w, *, sub, seq, tm=1024, tf=512):
    wg, wu, wd = w
    M, D = x2.shape
    F = wg.shape[-1]
    tpb = seq // tm
    n_i, n_j = M // tm, F // tf
    in_specs = [
        pl.BlockSpec((tm, D), lambda i, j: (i, 0)),
        pl.BlockSpec((None, N_SUB * N_MOD, D), lambda i, j: (i // tpb, 0, 0)),
        pl.BlockSpec((1, D), lambda i, j: (0, 0)),
        pl.BlockSpec((1, D), lambda i, j: (0, 0)),
        pl.BlockSpec((D, tf), lambda i, j: (0, j)),
        pl.BlockSpec((D, tf), lambda i, j: (0, j)),
        pl.BlockSpec((tf, D), lambda i, j: (j, 0)),
    ]
    out_specs = [pl.BlockSpec((tm, D), lambda i, j: (i, 0))]
    out_shape = [jax.ShapeDtypeStruct((M, D), F32)]
    operands = [x2, mods_l, gpre, gpost, wg, wu, wd]
    n_next = _add_cast_ahead(next_w, n_i * n_j, lambda i, j: i * n_j + j, in_specs, out_specs, out_shape, operands)
    outs = pl.pallas_call(
        functools.partial(_ffn_kernel, sub=sub, res_w=0.5, n_next=n_next),
        grid=(n_i, n_j),
        in_specs=in_specs,
        out_specs=out_specs,
        out_shape=out_shape,
        scratch_shapes=[pltpu.VMEM((tm, D), BF16)],
        compiler_params=_cparams(("arbitrary", "arbitrary")),
    )(*operands)
    return outs[0], tuple(outs[1:])


def _inproj_kernel(x_ref, mod_ref, gpre_ref, w_ref, *rest, heads_per_tile, tiles_per_section, n_next):
    next_f32, o_ref, next_bf16, h_ref = rest[:n_next], rest[n_next], rest[n_next + 1:-1], rest[-1]
    n = pl.program_id(1)

    @pl.when(n == 0)
    def _():
        _modulated_prenorm(x_ref, gpre_ref, mod_ref, 1, h_ref)

    res = _dot(h_ref[...], w_ref[...])
    res = res * jnp.where((n // tiles_per_section) % 3 == 0, QK_SCALE, 1.0)
    _cast_blocks(next_f32, next_bf16)
    for hh in range(heads_per_tile):
        o_ref[hh] = res[:, hh * HEAD_DIM:(hh + 1) * HEAD_DIM].astype(BF16)


def _inproj(x2, mods_l, gpre, w_in, next_w, *, batch, seq, tm=1024, tn=1024):
    M, D = x2.shape
    N = w_in.shape[-1]
    section = N // 6
    tn = min(tn, section)
    assert section % tn == 0
    tpb = seq // tm
    hpt = tn // HEAD_DIM
    n_i, n_n = M // tm, N // tn
    in_specs = [
        pl.BlockSpec((tm, D), lambda i, n: (i, 0)),
        pl.BlockSpec((None, N_SUB * N_MOD, D), lambda i, n: (i // tpb, 0, 0)),
        pl.BlockSpec((1, D), lambda i, n: (0, 0)),
        pl.BlockSpec((D, tn), lambda i, n: (0, n)),
    ]
    out_specs = [pl.BlockSpec((None, hpt, tm, HEAD_DIM), lambda i, n: (i // tpb, n, i % tpb, 0))]
    out_shape = [jax.ShapeDtypeStruct((batch, N // HEAD_DIM, seq, HEAD_DIM), BF16)]
    operands = [x2, mods_l, gpre, w_in]
    n_next = _add_cast_ahead(next_w, n_i * n_n, lambda i, n: i * n_n + n, in_specs, out_specs, out_shape, operands)
    outs = pl.pallas_call(
        functools.partial(_inproj_kernel, heads_per_tile=hpt, tiles_per_section=section // tn, n_next=n_next),
        grid=(n_i, n_n),
        in_specs=in_specs,
        out_specs=out_specs,
        out_shape=out_shape,
        scratch_shapes=[pltpu.VMEM((tm, D), BF16)],
        compiler_params=_cparams(("arbitrary", "arbitrary")),
    )(*operands)
    return outs[0], tuple(outs[1:])


def _t5_bucket_np(dist):
    n = np.maximum(dist, 0)
    max_exact = REL_BUCKETS // 2
    nf = np.maximum(n, 1).astype(np.float32)
    large = max_exact + (np.log(nf / np.float32(max_exact)) / np.float32(math.log(REL_MAX_DIST / max_exact))
                         * np.float32(REL_BUCKETS - max_exact)).astype(np.int32)
    large = np.minimum(large, REL_BUCKETS - 1)
    return np.where(n < max_exact, n, large).astype(np.int32)


def _bias_table_kernel(rb_ref, bucket_ref, o_ref):
    h = pl.program_id(0)
    for t in range(2):
        bucket = bucket_ref[t]
        tab = jnp.zeros(bucket.shape, F32)
        for b in range(REL_BUCKETS):
            tab = jnp.where(bucket == b, rb_ref[h, b], tab)
        if t == 0:
            row = lax.broadcasted_iota(jnp.int32, bucket.shape, 0)
            col = lax.broadcasted_iota(jnp.int32, bucket.shape, 1)
            tab = jnp.where(row >= col, tab, -jnp.inf)
        o_ref[t] = tab * LOG2E


def _bias_tables(rel_bias_t):
    H = rel_bias_t.shape[0]
    r = np.arange(MOBA_BLOCK)
    rel = r[:, None] - r[None, :]
    buckets = jnp.asarray(np.stack([_t5_bucket_np(rel), _t5_bucket_np(rel + MOBA_BLOCK)]))
    return pl.pallas_call(
        _bias_table_kernel,
        grid=(H,),
        in_specs=[
            pl.BlockSpec(memory_space=pltpu.SMEM),
            pl.BlockSpec((2, MOBA_BLOCK, MOBA_BLOCK), lambda h: (0, 0, 0)),
        ],
        out_specs=pl.BlockSpec((None, 2, MOBA_BLOCK, MOBA_BLOCK), lambda h: (h, 0, 0, 0)),
        out_shape=jax.ShapeDtypeStruct((H, 2, MOBA_BLOCK, MOBA_BLOCK), F32),
        compiler_params=_cparams(("arbitrary",)),
    )(rel_bias_t, buckets)


def _head_out(acc, gain):
    return (_rms(acc) * gain).astype(BF16)


def _moba_stages(h, rb_ref, q_ref, k_ref, v_ref, tab_ref, gain_ref, o_ref, s_ref, p_ref, v1_ref, *, nb):
    BLK = MOBA_BLOCK
    HALF = BLK // 2
    GATE_ROWS = 16
    assert nb <= GATE_ROWS
    b_far = rb_ref[h, REL_BUCKETS - 1] * LOG2E

    v1_ref[:, :HEAD_DIM] = v_ref[...]
    v1_ref[:, HEAD_DIM:] = jnp.ones((nb * BLK, HEAD_DIM), BF16)

    kf = k_ref[...].astype(F32).reshape(nb, BLK, HEAD_DIM)
    k_mean = jnp.mean(kf, axis=1)
    k_mean = jnp.concatenate([k_mean, jnp.zeros((GATE_ROWS - nb, HEAD_DIM), F32)], axis=0).astype(BF16)
    blk_id = lax.broadcasted_iota(jnp.int32, (GATE_ROWS, BLK), 0)

    def scores(i):
        qi = q_ref[i * BLK:(i + 1) * BLK, :]
        s_buf = s_ref.at[i % 2]
        if i > 0:
            gate = jnp.where(blk_id < i, _dot_nt(k_mean, qi), -jnp.inf)
            rank = jnp.zeros((GATE_ROWS, BLK), jnp.int32)
            for jp in range(i):
                cj = gate[jp:jp + 1, :]
                beats = (cj > gate) | ((cj == gate) & (jp < blk_id))
                rank = rank + beats.astype(jnp.int32)
            sel_t = jnp.where((rank < MOBA_TOPK) & (blk_id < i), 0.0, -jnp.inf)
            selw = jnp.concatenate([sel_t, jnp.zeros((HEAD_DIM - GATE_ROWS, BLK), F32)], axis=0).T

        m_run = None
        for j in range(i + 1):
            s = _dot_nt(qi, k_ref[j * BLK:(j + 1) * BLK, :])
            if j == i:
                s = s + tab_ref[0]
            elif j == i - 1:
                s = s + tab_ref[1] + selw[:, j:j + 1]
            else:
                s = s + (selw[:, j:j + 1] + b_far)
            s_buf[:, j * BLK:(j + 1) * BLK] = s
            mj = jnp.maximum(s[:, :HALF], s[:, HALF:])
            m_run = mj if m_run is None else jnp.maximum(m_run, mj)
        return jnp.max(m_run, axis=-1, keepdims=True)

    def output(i, m):
        s_buf = s_ref.at[i % 2]
        p_buf = p_ref.at[i % 2]
        for j in range(i + 1):
            p_buf[:, j * BLK:(j + 1) * BLK] = jnp.exp2(s_buf[:, j * BLK:(j + 1) * BLK] - m).astype(BF16)
        acc = _dot(p_buf[:, :(i + 1) * BLK], v1_ref[:(i + 1) * BLK, :])
        o_ref[i * BLK:(i + 1) * BLK, :] = _head_out(acc[:, :HEAD_DIM] / acc[:, HEAD_DIM:HEAD_DIM + 1],
                                                    gain_ref[...])

    return scores, output


def _sb_stages(q_ref, k_ref, v_ref, gain_ref, o_ref, acc_ref, rem_ref, *, nb):
    BLK = SB_BLOCK
    row = lax.broadcasted_iota(jnp.int32, (BLK, BLK), 0)
    col = lax.broadcasted_iota(jnp.int32, (BLK, BLK), 1)
    strict = col < row
    after = jnp.where(row > col, 1.0, 0.0).astype(BF16)

    def tail_sums(x):
        return _dot(x.astype(BF16), after)

    def strip_logits(qi, k_strip, diagonal):
        n = k_strip.shape[0] // BLK
        z = _dot_nt(qi, k_strip)
        t = jnp.exp2(z)
        e = jnp.minimum(t, 1.0 / t)
        fail = jnp.maximum(z, 0.0) + jnp.log2(1.0 + e)
        parts = []
        for b in range(n):
            cols = slice(b * BLK, (b + 1) * BLK)
            f_b = fail[:, cols]
            f_m = jnp.where(strict, f_b, 0.0) if diagonal and b == n - 1 else f_b
            w = tail_sums(f_m)
            parts.append((z[:, cols] - f_b - w, w[:, 0:1] + f_m[:, 0:1]))
        return parts

    def strip_output(parts, v_strip, rem, diagonal):
        n = len(parts)
        weights = [None] * n
        for b in reversed(range(n)):
            log_a, spent = parts[b]
            a = jnp.exp2(log_a if rem is None else log_a + rem)
            if diagonal and b == n - 1:
                a = jnp.where(strict, a, 0.0)
            weights[b] = a.astype(BF16)
            rem = -spent if rem is None else rem - spent
        a_all = weights[0] if n == 1 else jnp.concatenate(weights, axis=1)
        return _dot(a_all, v_strip), rem

    def strip(qi, k_strip, v_strip, rem, diagonal):
        return strip_output(strip_logits(qi, k_strip, diagonal), v_strip, rem, diagonal)

    def alive(rem):
        return (jnp.max(rem) > SB_LOG_ZERO * LOG2E).astype(jnp.int32)

    def near_keys(i):
        return slice(max(0, i - (SB_NEAR_BLOCKS - 1)) * BLK, (i + 1) * BLK)

    pending = []

    def logits(i):
        return strip_logits(q_ref[i * BLK:(i + 1) * BLK, :], k_ref[near_keys(i), :], True)

    def output(i, parts):
        acc, rem = strip_output(parts, v_ref[near_keys(i), :], None, True)
        if near_keys(i).start == 0:
            o_ref[i * BLK:(i + 1) * BLK, :] = _head_out(acc, gain_ref[...])
        else:
            acc_ref[i] = acc
            rem_ref[i] = jnp.broadcast_to(rem, (BLK, HEAD_DIM))
            pending.append((i, near_keys(i).start // BLK - 1, alive(rem)))

    def older():
        for i, next_blk, alive0 in pending:
            qi = q_ref[i * BLK:(i + 1) * BLK, :]

            def trip(c, n, qi=qi):
                j, _, rem, acc = c
                keys = pl.ds(pl.multiple_of((j - (n - 1)) * BLK, BLK), n * BLK)
                add, rem = strip(qi, k_ref[keys, :], v_ref[keys, :], rem, False)
                return j - n, alive(rem), rem, acc + add

            state = (jnp.int32(next_blk), alive0, rem_ref[i][:, 0:1], acc_ref[i])
            if next_blk >= 1:
                state = lax.while_loop(lambda c: (c[0] >= 1) & (c[1] > 0), functools.partial(trip, n=2), state)
            state = lax.while_loop(lambda c: (c[0] >= 0) & (c[1] > 0), functools.partial(trip, n=1), state)
            o_ref[i * BLK:(i + 1) * BLK, :] = _head_out(state[3], gain_ref[...])

    return logits, output, older


def _attn_kernel(rb_ref, qa_ref, ka_ref, va_ref, qb_ref, kb_ref, vb_ref, tab_ref, gain_a_ref, gain_b_ref,
                 oa_ref, ob_ref, s_ref, p_ref, v1_ref, acc_ref, rem_ref, *, nb):
    g = pl.program_id(1)
    a_scores, a_output = _moba_stages(g, rb_ref, qa_ref, ka_ref, va_ref, tab_ref, gain_a_ref, oa_ref,
                                      s_ref, p_ref, v1_ref, nb=nb)
    b_logits, b_output, b_older = _sb_stages(qb_ref, kb_ref, vb_ref, gain_b_ref, ob_ref, acc_ref, rem_ref, nb=nb)
    m, parts = a_scores(0), b_logits(0)
    for i in range(nb):
        if i + 1 < nb:
            m_next, parts_next = a_scores(i + 1), b_logits(i + 1)
        a_output(i, m)
        b_output(i, parts)
        m, parts = m_next, parts_next
    b_older()


def _attention(proj, rel_bias_t, tables, gain2, *, n_heads):
    B, _, S, _ = proj.shape
    nb = S // MOBA_BLOCK
    blk = (None, None, S, HEAD_DIM)

    def col_head(which):
        return pl.BlockSpec(blk, lambda b, g: (b, which * n_heads + g, 0, 0))

    out_spec = pl.BlockSpec((None, S, HEAD_DIM), lambda b, g: (b, 0, g))
    out_shape = jax.ShapeDtypeStruct((B, S, n_heads * HEAD_DIM), BF16)
    return pl.pallas_call(
        functools.partial(_attn_kernel, nb=nb),
        grid=(B, n_heads),
        in_specs=[
            pl.BlockSpec(memory_space=pltpu.SMEM),
            *[col_head(which) for which in range(6)],
            pl.BlockSpec((None, 2, MOBA_BLOCK, MOBA_BLOCK), lambda b, g: (g, 0, 0, 0)),
            pl.BlockSpec((1, HEAD_DIM), lambda b, g: (0, g)),
            pl.BlockSpec((1, HEAD_DIM), lambda b, g: (0, n_heads + g)),
        ],
        out_specs=[out_spec, out_spec],
        out_shape=[out_shape, out_shape],
        scratch_shapes=[
            pltpu.VMEM((2, MOBA_BLOCK, S), F32),
            pltpu.VMEM((2, MOBA_BLOCK, S), BF16),
            pltpu.VMEM((S, 2 * HEAD_DIM), BF16),
            pltpu.VMEM((nb, SB_BLOCK, HEAD_DIM), F32),
            pltpu.VMEM((nb, SB_BLOCK, HEAD_DIM), F32),
        ],
        compiler_params=_cparams(("parallel", "parallel")),
    )(rel_bias_t, *([proj] * 6), tables, gain2, gain2)


def _outproj_kernel(x_ref, oa_ref, ob_ref, mod_ref, gpost_ref, wa_ref, wb_ref, *rest, n_next):
    next_f32, out_ref, next_bf16 = rest[:n_next], rest[n_next], rest[n_next + 1:]
    out_ref[...] = _dot(oa_ref[...], wa_ref[...]) + _dot(ob_ref[...], wb_ref[...])
    _cast_blocks(next_f32, next_bf16)
    _gated_postnorm_residual(x_ref, out_ref, gpost_ref, mod_ref, 1, 1.0, out_ref)


def _outproj(x2, oa2, ob2, mods_l, gpost, w_out, next_w, *, seq, tm=512):
    M, D = x2.shape
    K = oa2.shape[-1]
    tpb = seq // tm
    in_specs = [
        pl.BlockSpec((tm, D), lambda i: (i, 0)),
        pl.BlockSpec((tm, K), lambda i: (i, 0)),
        pl.BlockSpec((tm, K), lambda i: (i, 0)),
        pl.BlockSpec((None, N_SUB * N_MOD, D), lambda i: (i // tpb, 0, 0)),
        pl.BlockSpec((1, D), lambda i: (0, 0)),
        pl.BlockSpec((K, D), lambda i: (0, 0)),
        pl.BlockSpec((K, D), lambda i: (1, 0)),
    ]
    out_specs = [pl.BlockSpec((tm, D), lambda i: (i, 0))]
    out_shape = [jax.ShapeDtypeStruct((M, D), F32)]
    operands = [x2, oa2, ob2, mods_l, gpost, w_out, w_out]
    n_next = _add_cast_ahead(next_w, M // tm, lambda i: i, in_specs, out_specs, out_shape, operands)
    outs = pl.pallas_call(
        functools.partial(_outproj_kernel, n_next=n_next),
        grid=(M // tm,),
        in_specs=in_specs,
        out_specs=out_specs,
        out_shape=out_shape,
        compiler_params=_cparams(("arbitrary",)),
    )(*operands)
    return outs[0], tuple(outs[1:])


def kernel(x, c, ada_w, ada_b, norm_pre, norm_post, ffn1_w_gate, ffn1_w_up, ffn1_w_down,
           mix_w_in, mix_out_gain, mix_w_out, rel_bias, ffn2_w_gate, ffn2_w_up, ffn2_w_down):
    B, S, D = x.shape
    L = ada_w.shape[0]
    n_heads = rel_bias.shape[1]
    d_mix = mix_w_out.shape[1]
    d_grp = n_heads * HEAD_DIM
    assert S % MOBA_BLOCK == 0 and d_mix == 2 * d_grp and mix_w_in.shape[-1] == 3 * d_mix

    ffn1 = [ffn1_w_gate, ffn1_w_up, ffn1_w_down]
    ffn2 = [ffn2_w_gate, ffn2_w_up, ffn2_w_down]
    assert B <= SUBLANES
    c_pad = jnp.pad(c, ((0, SUBLANES - B), (0, 0)))
    mods, (*w1, w_in) = _ada_mods(c_pad, ada_w, ada_b, (0, ffn1 + [mix_w_in]))
    mods = mods[:, :B].reshape(L, B, N_SUB * N_MOD, D)
    rel_bias_t = rel_bias.T
    tables = _bias_tables(rel_bias_t)

    x2 = x.reshape(B * S, D)
    for l in range(L):
        gpre = norm_pre[l].reshape(N_SUB, 1, D)
        gpost = norm_post[l].reshape(N_SUB, 1, D)
        gain2 = mix_out_gain[l].reshape(1, d_mix)
        x2, w2 = _ffn_sublayer(x2, mods[l], gpre[0], gpost[0], w1, (l, ffn2), sub=0, seq=S)
        proj, (w_out,) = _inproj(x2, mods[l], gpre[1], w_in, (l, [mix_w_out]), batch=B, seq=S)
        o_a, o_b = _attention(proj, rel_bias_t, tables, gain2, n_heads=n_heads)
        x2, w_in = _outproj(x2, o_a.reshape(B * S, d_grp), o_b.reshape(B * S, d_grp), mods[l], gpost[1], w_out,
                            (l + 1, [mix_w_in]) if l + 1 < L else None, seq=S)
        w_in = w_in[0] if w_in else None
        x2, w1 = _ffn_sublayer(x2, mods[l], gpre[2], gpost[2], w2, (l + 1, ffn1) if l + 1 < L else None,
                               sub=2, seq=S)
    return x2.reshape(B, S, D)
```

```python
import functools
import math

import numpy as np
import jax
import jax.numpy as jnp
from jax import lax
from jax.experimental import pallas as pl
from jax.experimental.pallas import tpu as pltpu

F32 = jnp.float32
BF16 = jnp.bfloat16

HEAD_DIM = 128
MOBA_BLOCK = 256
MOBA_TOPK = 3
REL_BUCKETS = 32
REL_MAX_DIST = 128
N_SUB = 3
N_MOD = 3
EPS = 1e-6
SB_BLOCK = 256
SB_LOG_ZERO = -104.0
SB_NEAR_BLOCKS = 3
LOG2E = math.log2(math.e)
FFN_DOWN_COLS = 512
QK_SCALE = HEAD_DIM ** -0.5 * LOG2E
SUBLANES = 8
ROW_CHUNK = 2 * SUBLANES
ROW_UNROLL = 8

V7X_VMEM_BYTES = 64 * 1024 * 1024
VMEM_LIMIT = V7X_VMEM_BYTES - 2 * 1024 * 1024


def _cparams(sem):
    return pltpu.CompilerParams(dimension_semantics=sem, vmem_limit_bytes=VMEM_LIMIT)


def _dot(a, b):
    return jnp.dot(a, b, preferred_element_type=F32)


def _dot_nt(a, b):
    return lax.dot_general(a, b, (((1,), (1,)), ((), ())), preferred_element_type=F32)


def _rms(x):
    return x * lax.rsqrt(jnp.mean(x * x, axis=-1, keepdims=True) + EPS)


def _row_chunks(n_rows, body):
    def step(r, carry):
        body(pl.ds(pl.multiple_of(r * ROW_CHUNK, ROW_CHUNK), ROW_CHUNK))
        return carry

    lax.fori_loop(0, n_rows // ROW_CHUNK, step, 0, unroll=ROW_UNROLL)


def _modulated_prenorm(x_ref, gpre_ref, mod_ref, sub, h_ref, zero_ref=None):
    gain = gpre_ref[...] * (1.0 + mod_ref[3 * sub + 1:3 * sub + 2, :])
    shift = mod_ref[3 * sub:3 * sub + 1, :]

    def body(rows):
        h_ref[rows, :] = (_rms(x_ref[rows, :]) * gain + shift).astype(BF16)
        if zero_ref is not None:
            zero_ref[rows, :] = jnp.zeros((ROW_CHUNK, zero_ref.shape[1]), zero_ref.dtype)

    _row_chunks(x_ref.shape[0], body)


def _gated_postnorm_residual(x_ref, y_ref, gpost_ref, mod_ref, sub, res_w, o_ref):
    coef = (res_w * mod_ref[3 * sub + 2:3 * sub + 3, :]) * gpost_ref[...]

    for r in range(0, x_ref.shape[0], SUBLANES):
        rows = slice(r, r + SUBLANES)
        o_ref[rows, :] = x_ref[rows, :] + _rms(y_ref[rows, :]) * coef


def _ada_kernel(c_ref, w_ref, b_ref, *rest, n_next):
    next_f32, o_ref, next_bf16 = rest[:n_next], rest[n_next], rest[n_next + 1:]
    c = c_ref[...]
    ca = (c * jax.nn.sigmoid(c)).astype(BF16)
    o_ref[...] = _dot(ca, w_ref[...].astype(BF16)) + b_ref[...]
    _cast_blocks(next_f32, next_bf16)


def _ada_mods(c_pad, ada_w, ada_b, next_w):
    L, D, N = ada_w.shape
    tn = min(1024, D)
    n_n = N // tn
    in_specs = [
        pl.BlockSpec((SUBLANES, D), lambda l, n: (0, 0)),
        pl.BlockSpec((None, D, tn), lambda l, n: (l, 0, n)),
        pl.BlockSpec((None, 1, tn), lambda l, n: (l, 0, n)),
    ]
    out_specs = [pl.BlockSpec((None, SUBLANES, tn), lambda l, n: (l, 0, n))]
    out_shape = [jax.ShapeDtypeStruct((L, SUBLANES, N), F32)]
    operands = [c_pad, ada_w, ada_b.reshape(L, 1, N)]
    n_next = _add_cast_ahead(next_w, L * n_n, lambda l, n: l * n_n + n, in_specs, out_specs, out_shape, operands)
    outs = pl.pallas_call(
        functools.partial(_ada_kernel, n_next=n_next),
        grid=(L, n_n),
        in_specs=in_specs,
        out_specs=out_specs,
        out_shape=out_shape,
        compiler_params=_cparams(("arbitrary", "arbitrary")),
    )(*operands)
    return outs[0], tuple(outs[1:])


def _ffn_kernel(x_ref, mod_ref, gpre_ref, gpost_ref, wg_ref, wu_ref, wd_ref, *rest, sub, res_w, n_next):
    next_f32, o_ref, next_bf16, h_ref = rest[:n_next], rest[n_next], rest[n_next + 1:-1], rest[-1]
    j = pl.program_id(1)

    @pl.when(j == 0)
    def _():
        _modulated_prenorm(x_ref, gpre_ref, mod_ref, sub, h_ref, zero_ref=o_ref)

    h = h_ref[...]
    g = _dot(h, wg_ref[...])
    u = _dot(h, wu_ref[...])
    a = (g * jax.nn.sigmoid(g) * u).astype(BF16)
    n_chunks = o_ref.shape[1] // FFN_DOWN_COLS
    for c in range(n_chunks):
        cols = slice(c * FFN_DOWN_COLS, (c + 1) * FFN_DOWN_COLS)
        o_ref[:, cols] += _dot(a, wd_ref[:, cols])
        if c == 0:
            _cast_blocks(next_f32, next_bf16)

    @pl.when(j == pl.num_programs(1) - 1)
    def _():
        _gated_postnorm_residual(x_ref, o_ref, gpost_ref, mod_ref, sub, res_w, o_ref)


def _cast_rows(n_rows, n_steps):
    for r in range(16, n_rows + 1, 16):
        if n_rows % r == 0 and n_rows // r <= n_steps:
            return r
    raise ValueError((n_rows, n_steps))


def _add_cast_ahead(next_w, n_steps, step_of, in_specs, out_specs, out_shape, operands):
    if next_w is None:
        return 0
    l_next, mats = next_w
    for m in mats:
        _, rows, cols = m.shape
        r = _cast_rows(rows, n_steps)
        last = rows // r - 1
        in_specs.append(pl.BlockSpec(
            (None, r, cols), lambda *g, last=last: (l_next, jnp.minimum(step_of(*g), last), 0)))
        out_specs.append(pl.BlockSpec((r, cols), lambda *g, last=last: (jnp.minimum(step_of(*g), last), 0)))
        out_shape.append(jax.ShapeDtypeStruct((rows, cols), BF16))
        operands.append(m)
    return len(mats)


def _cast_blocks(srcs, dsts):
    for src, dst in zip(srcs, dsts):
        dst[...] = src[...].astype(BF16)


def _ffn_sublayer(x2, mods_l, gpre, gpost, w, next_w, *, sub, seq, tm=1024, tf=512):
    wg, wu, wd = w
    M, D = x2.shape
    F = wg.shape[-1]
    tpb = seq // tm
    n_i, n_j = M // tm, F // tf
    in_specs = [
        pl.BlockSpec((tm, D), lambda i, j: (i, 0)),
        pl.BlockSpec((None, N_SUB * N_MOD, D), lambda i, j: (i // tpb, 0, 0)),
        pl.BlockSpec((1, D), lambda i, j: (0, 0)),
        pl.BlockSpec((1, D), lambda i, j: (0, 0)),
        pl.BlockSpec((D, tf), lambda i, j: (0, j)),
        pl.BlockSpec((D, tf), lambda i, j: (0, j)),
        pl.BlockSpec((tf, D), lambda i, j: (j, 0)),
    ]
    out_specs = [pl.BlockSpec((tm, D), lambda i, j: (i, 0))]
    out_shape = [jax.ShapeDtypeStruct((M, D), F32)]
    operands = [x2, mods_l, gpre, gpost, wg, wu, wd]
    n_next = _add_cast_ahead(next_w, n_i * n_j, lambda i, j: i * n_j + j, in_specs, out_specs, out_shape, operands)
    outs = pl.pallas_call(
        functools.partial(_ffn_kernel, sub=sub, res_w=0.5, n_next=n_next),
        grid=(n_i, n_j),
        in_specs=in_specs,
        out_specs=out_specs,
        out_shape=out_shape,
        scratch_shapes=[pltpu.VMEM((tm, D), BF16)],
        compiler_params=_cparams(("arbitrary", "arbitrary")),
    )(*operands)
    return outs[0], tuple(outs[1:])


def _inproj_kernel(x0_ref, xn_ref, mod0_ref, modn_ref, gpre_ref, w_ref, *rest,
                   heads_per_tile, tiles_per_section, n_next, n_chunks):
    next_f32, o_ref, next_bf16 = rest[:n_next], rest[n_next], rest[n_next + 1:-2]
    h_even, h_odd = rest[-2:]
    i, n = pl.program_id(0), pl.program_id(1)

    @pl.when((i == 0) & (n == 0))
    def _():
        _modulated_prenorm(x0_ref, gpre_ref, mod0_ref, 1, h_even)

    def step(h_cur, h_next):
        q_scale = jnp.where((n // tiles_per_section) % 3 == 0, QK_SCALE, 1.0)
        heads_half = heads_per_tile // 2

        def project(first_head, scale):
            cols = slice(first_head * HEAD_DIM, (first_head + heads_half) * HEAD_DIM)
            res = _dot(h_cur[...], w_ref[:, cols]) * scale
            for hh in range(heads_half):
                o_ref[first_head + hh] = res[:, hh * HEAD_DIM:(hh + 1) * HEAD_DIM].astype(BF16)

        project(0, q_scale)
        rows_ch = xn_ref.shape[0]
        base = jnp.minimum(n, n_chunks - 1) * rows_ch
        gain = gpre_ref[...] * (1.0 + modn_ref[4:5, :])
        shift = modn_ref[3:4, :]
        tie = jnp.zeros((SUBLANES, HEAD_DIM), F32)
        for r in range(0, rows_ch, ROW_CHUNK):
            y = _rms(xn_ref[r:r + ROW_CHUNK, :]) * gain + shift
            h_next[pl.ds(pl.multiple_of(base + r, ROW_CHUNK), ROW_CHUNK), :] = y.astype(BF16)
            tie = tie + y[:SUBLANES, :HEAD_DIM]
        _cast_blocks(next_f32, next_bf16)
        project(heads_half, q_scale + (tie * 0.0)[0:1, 0:1])

    @pl.when(i % 2 == 0)
    def _():
        step(h_even, h_odd)

    @pl.when(i % 2 == 1)
    def _():
        step(h_odd, h_even)


def _inproj(x2, mods_l, gpre, w_in, next_w, *, batch, seq, tm=1024, tn=1024):
    M, D = x2.shape
    N = w_in.shape[-1]
    section = N // 6
    tn = min(tn, section)
    assert section % tn == 0
    tpb = seq // tm
    hpt = tn // HEAD_DIM
    n_i, n_n = M // tm, N // tn
    n_chunks = 1 << (n_n.bit_length() - 1)
    rows_ch = tm // n_chunks
    assert rows_ch % ROW_CHUNK == 0

    def next_tile(i):
        return jnp.minimum(i + 1, n_i - 1)

    in_specs = [
        pl.BlockSpec((tm, D), lambda i, n: (0, 0)),
        pl.BlockSpec((rows_ch, D), lambda i, n: (next_tile(i) * n_chunks + jnp.minimum(n, n_chunks - 1), 0)),
        pl.BlockSpec((None, N_SUB * N_MOD, D), lambda i, n: (0, 0, 0)),
        pl.BlockSpec((None, N_SUB * N_MOD, D), lambda i, n: (next_tile(i) // tpb, 0, 0)),
        pl.BlockSpec((1, D), lambda i, n: (0, 0)),
        pl.BlockSpec((D, tn), lambda i, n: (0, n)),
    ]
    out_specs = [pl.BlockSpec((None, hpt, tm, HEAD_DIM), lambda i, n: (i // tpb, n, i % tpb, 0))]
    out_shape = [jax.ShapeDtypeStruct((batch, N // HEAD_DIM, seq, HEAD_DIM), BF16)]
    operands = [x2, x2, mods_l, mods_l, gpre, w_in]
    n_next = _add_cast_ahead(next_w, n_i * n_n, lambda i, n: i * n_n + n, in_specs, out_specs, out_shape, operands)
    outs = pl.pallas_call(
        functools.partial(_inproj_kernel, heads_per_tile=hpt, tiles_per_section=section // tn, n_next=n_next,
                          n_chunks=n_chunks),
        grid=(n_i, n_n),
        in_specs=in_specs,
        out_specs=out_specs,
        out_shape=out_shape,
        scratch_shapes=[pltpu.VMEM((tm, D), BF16), pltpu.VMEM((tm, D), BF16)],
        compiler_params=_cparams(("arbitrary", "arbitrary")),
    )(*operands)
    return outs[0], tuple(outs[1:])


def _t5_bucket_np(dist):
    n = np.maximum(dist, 0)
    max_exact = REL_BUCKETS // 2
    nf = np.maximum(n, 1).astype(np.float32)
    large = max_exact + (np.log(nf / np.float32(max_exact)) / np.float32(math.log(REL_MAX_DIST / max_exact))
                         * np.float32(REL_BUCKETS - max_exact)).astype(np.int32)
    large = np.minimum(large, REL_BUCKETS - 1)
    return np.where(n < max_exact, n, large).astype(np.int32)


def _bias_table_kernel(rb_ref, bucket_ref, o_ref):
    h = pl.program_id(0)
    for t in range(2):
        bucket = bucket_ref[t]
        tab = jnp.zeros(bucket.shape, F32)
        for b in range(REL_BUCKETS):
            tab = jnp.where(bucket == b, rb_ref[h, b], tab)
        if t == 0:
            row = lax.broadcasted_iota(jnp.int32, bucket.shape, 0)
            col = lax.broadcasted_iota(jnp.int32, bucket.shape, 1)
            tab = jnp.where(row >= col, tab, -jnp.inf)
        o_ref[t] = tab * LOG2E


def _bias_tables(rel_bias_t):
    H = rel_bias_t.shape[0]
    r = np.arange(MOBA_BLOCK)
    rel = r[:, None] - r[None, :]
    buckets = jnp.asarray(np.stack([_t5_bucket_np(rel), _t5_bucket_np(rel + MOBA_BLOCK)]))
    return pl.pallas_call(
        _bias_table_kernel,
        grid=(H,),
        in_specs=[
            pl.BlockSpec(memory_space=pltpu.SMEM),
            pl.BlockSpec((2, MOBA_BLOCK, MOBA_BLOCK), lambda h: (0, 0, 0)),
        ],
        out_specs=pl.BlockSpec((None, 2, MOBA_BLOCK, MOBA_BLOCK), lambda h: (h, 0, 0, 0)),
        out_shape=jax.ShapeDtypeStruct((H, 2, MOBA_BLOCK, MOBA_BLOCK), F32),
        compiler_params=_cparams(("arbitrary",)),
    )(rel_bias_t, buckets)


def _head_out(acc, gain):
    return (_rms(acc) * gain).astype(BF16)


def _moba_stages(h, rb_ref, q_ref, k_ref, v_ref, tab_ref, gain_ref, o_ref, s_ref, p_ref, v1_ref, *, nb):
    BLK = MOBA_BLOCK
    HALF = BLK // 2
    GATE_ROWS = 16
    assert nb <= GATE_ROWS
    b_far = rb_ref[h, REL_BUCKETS - 1] * LOG2E

    v1_ref[:, :HEAD_DIM] = v_ref[...]
    v1_ref[:, HEAD_DIM:] = jnp.ones((nb * BLK, HEAD_DIM), BF16)

    kf = k_ref[...].astype(F32).reshape(nb, BLK, HEAD_DIM)
    k_mean = jnp.mean(kf, axis=1)
    k_mean = jnp.concatenate([k_mean, jnp.zeros((GATE_ROWS - nb, HEAD_DIM), F32)], axis=0).astype(BF16)
    blk_id = lax.broadcasted_iota(jnp.int32, (GATE_ROWS, BLK), 0)

    def scores(i):
        qi = q_ref[i * BLK:(i + 1) * BLK, :]
        s_buf = s_ref.at[i % 2]
        if i > 0:
            gate = jnp.where(blk_id < i, _dot_nt(k_mean, qi), -jnp.inf)
            rank = jnp.zeros((GATE_ROWS, BLK), jnp.int32)
            for jp in range(i):
                cj = gate[jp:jp + 1, :]
                beats = (cj > gate) | ((cj == gate) & (jp < blk_id))
                rank = rank + beats.astype(jnp.int32)
            sel_t = jnp.where((rank < MOBA_TOPK) & (blk_id < i), 0.0, -jnp.inf)
            selw = jnp.concatenate([sel_t, jnp.zeros((HEAD_DIM - GATE_ROWS, BLK), F32)], axis=0).T

        m_run = None
        for j in range(i + 1):
            s = _dot_nt(qi, k_ref[j * BLK:(j + 1) * BLK, :])
            if j == i:
                s = s + tab_ref[0]
            elif j == i - 1:
                s = s + tab_ref[1] + selw[:, j:j + 1]
            else:
                s = s + (selw[:, j:j + 1] + b_far)
            s_buf[:, j * BLK:(j + 1) * BLK] = s
            mj = jnp.maximum(s[:, :HALF], s[:, HALF:])
            m_run = mj if m_run is None else jnp.maximum(m_run, mj)
        return jnp.max(m_run, axis=-1, keepdims=True)

    def output(i, m):
        s_buf = s_ref.at[i % 2]
        p_buf = p_ref.at[i % 2]
        for j in range(i + 1):
            p_buf[:, j * BLK:(j + 1) * BLK] = jnp.exp2(s_buf[:, j * BLK:(j + 1) * BLK] - m).astype(BF16)
        acc = _dot(p_buf[:, :(i + 1) * BLK], v1_ref[:(i + 1) * BLK, :])
        o_ref[i * BLK:(i + 1) * BLK, :] = _head_out(acc[:, :HEAD_DIM] / acc[:, HEAD_DIM:HEAD_DIM + 1],
                                                    gain_ref[...])

    return scores, output


def _sb_stages(q_ref, k_ref, v_ref, gain_ref, o_ref, acc_ref, rem_ref, *, nb):
    BLK = SB_BLOCK
    row = lax.broadcasted_iota(jnp.int32, (BLK, BLK), 0)
    col = lax.broadcasted_iota(jnp.int32, (BLK, BLK), 1)
    strict = col < row
    after = jnp.where(row > col, 1.0, 0.0).astype(BF16)

    def tail_sums(x):
        return _dot(x.astype(BF16), after)

    def strip_logits(qi, k_strip, diagonal):
        n = k_strip.shape[0] // BLK
        z = _dot_nt(qi, k_strip)
        t = jnp.exp2(z)
        e = jnp.minimum(t, 1.0 / t)
        fail = jnp.maximum(z, 0.0) + jnp.log2(1.0 + e)
        parts = []
        for b in range(n):
            cols = slice(b * BLK, (b + 1) * BLK)
            f_b = fail[:, cols]
            f_m = jnp.where(strict, f_b, 0.0) if diagonal and b == n - 1 else f_b
            w = tail_sums(f_m)
            parts.append((z[:, cols] - f_b - w, w[:, 0:1] + f_m[:, 0:1]))
        return parts

    def strip_output(parts, v_strip, rem, diagonal):
        n = len(parts)
        weights = [None] * n
        for b in reversed(range(n)):
            log_a, spent = parts[b]
            a = jnp.exp2(log_a if rem is None else log_a + rem)
            if diagonal and b == n - 1:
                a = jnp.where(strict, a, 0.0)
            weights[b] = a.astype(BF16)
            rem = -spent if rem is None else rem - spent
        a_all = weights[0] if n == 1 else jnp.concatenate(weights, axis=1)
        return _dot(a_all, v_strip), rem

    def strip(qi, k_strip, v_strip, rem, diagonal):
        return strip_output(strip_logits(qi, k_strip, diagonal), v_strip, rem, diagonal)

    def alive(rem):
        return (jnp.max(rem) > SB_LOG_ZERO * LOG2E).astype(jnp.int32)

    def near_keys(i):
        return slice(max(0, i - (SB_NEAR_BLOCKS - 1)) * BLK, (i + 1) * BLK)

    pending = []

    def logits(i):
        return strip_logits(q_ref[i * BLK:(i + 1) * BLK, :], k_ref[near_keys(i), :], True)

    def output(i, parts):
        acc, rem = strip_output(parts, v_ref[near_keys(i), :], None, True)
        if near_keys(i).start == 0:
            o_ref[i * BLK:(i + 1) * BLK, :] = _head_out(acc, gain_ref[...])
        else:
            acc_ref[i] = acc
            rem_ref[i] = jnp.broadcast_to(rem, (BLK, HEAD_DIM))
            pending.append((i, near_keys(i).start // BLK - 1, alive(rem)))

    def older():
        for i, next_blk, alive0 in pending:
            qi = q_ref[i * BLK:(i + 1) * BLK, :]

            def trip(c, n, qi=qi):
                j, _, rem, acc = c
                keys = pl.ds(pl.multiple_of((j - (n - 1)) * BLK, BLK), n * BLK)
                add, rem = strip(qi, k_ref[keys, :], v_ref[keys, :], rem, False)
                return j - n, alive(rem), rem, acc + add

            state = (jnp.int32(next_blk), alive0, rem_ref[i][:, 0:1], acc_ref[i])
            if next_blk >= 1:
                state = lax.while_loop(lambda c: (c[0] >= 1) & (c[1] > 0), functools.partial(trip, n=2), state)
            state = lax.while_loop(lambda c: (c[0] >= 0) & (c[1] > 0), functools.partial(trip, n=1), state)
            o_ref[i * BLK:(i + 1) * BLK, :] = _head_out(state[3], gain_ref[...])

    return logits, output, older


def _attn_kernel(rb_ref, qa_ref, ka_ref, va_ref, qb_ref, kb_ref, vb_ref, tab_ref, gain_a_ref, gain_b_ref,
                 oa_ref, ob_ref, s_ref, p_ref, v1_ref, acc_ref, rem_ref, *, nb):
    g = pl.program_id(1)
    a_scores, a_output = _moba_stages(g, rb_ref, qa_ref, ka_ref, va_ref, tab_ref, gain_a_ref, oa_ref,
                                      s_ref, p_ref, v1_ref, nb=nb)
    b_logits, b_output, b_older = _sb_stages(qb_ref, kb_ref, vb_ref, gain_b_ref, ob_ref, acc_ref, rem_ref, nb=nb)
    m, parts = a_scores(0), b_logits(0)
    for i in range(nb):
        if i + 1 < nb:
            m_next, parts_next = a_scores(i + 1), b_logits(i + 1)
        a_output(i, m)
        b_output(i, parts)
        m, parts = m_next, parts_next
    b_older()


def _attention(proj, rel_bias_t, tables, gain2, *, n_heads):
    B, _, S, _ = proj.shape
    nb = S // MOBA_BLOCK
    blk = (None, None, S, HEAD_DIM)

    def col_head(which):
        return pl.BlockSpec(blk, lambda b, g: (b, which * n_heads + g, 0, 0))

    out_spec = pl.BlockSpec((None, S, HEAD_DIM), lambda b, g: (b, 0, g))
    out_shape = jax.ShapeDtypeStruct((B, S, n_heads * HEAD_DIM), BF16)
    return pl.pallas_call(
        functools.partial(_attn_kernel, nb=nb),
        grid=(B, n_heads),
        in_specs=[
            pl.BlockSpec(memory_space=pltpu.SMEM),
            *[col_head(which) for which in range(6)],
            pl.BlockSpec((None, 2, MOBA_BLOCK, MOBA_BLOCK), lambda b, g: (g, 0, 0, 0)),
            pl.BlockSpec((1, HEAD_DIM), lambda b, g: (0, g)),
            pl.BlockSpec((1, HEAD_DIM), lambda b, g: (0, n_heads + g)),
        ],
        out_specs=[out_spec, out_spec],
        out_shape=[out_shape, out_shape],
        scratch_shapes=[
            pltpu.VMEM((2, MOBA_BLOCK, S), F32),
            pltpu.VMEM((2, MOBA_BLOCK, S), BF16),
            pltpu.VMEM((S, 2 * HEAD_DIM), BF16),
            pltpu.VMEM((nb, SB_BLOCK, HEAD_DIM), F32),
            pltpu.VMEM((nb, SB_BLOCK, HEAD_DIM), F32),
        ],
        compiler_params=_cparams(("parallel", "parallel")),
    )(rel_bias_t, *([proj] * 6), tables, gain2, gain2)


def _outproj_kernel(x_ref, oa_ref, ob_ref, mod_ref, gpost_ref, wa_ref, wb_ref, *rest, n_next):
    next_f32, out_ref, next_bf16 = rest[:n_next], rest[n_next], rest[n_next + 1:]
    out_ref[...] = _dot(oa_ref[...], wa_ref[...]) + _dot(ob_ref[...], wb_ref[...])
    _cast_blocks(next_f32, next_bf16)
    _gated_postnorm_residual(x_ref, out_ref, gpost_ref, mod_ref, 1, 1.0, out_ref)


def _outproj(x2, oa2, ob2, mods_l, gpost, w_out, next_w, *, seq, tm=512):
    M, D = x2.shape
    K = oa2.shape[-1]
    tpb = seq // tm
    in_specs = [
        pl.BlockSpec((tm, D), lambda i: (i, 0)),
        pl.BlockSpec((tm, K), lambda i: (i, 0)),
        pl.BlockSpec((tm, K), lambda i: (i, 0)),
        pl.BlockSpec((None, N_SUB * N_MOD, D), lambda i: (i // tpb, 0, 0)),
        pl.BlockSpec((1, D), lambda i: (0, 0)),
        pl.BlockSpec((K, D), lambda i: (0, 0)),
        pl.BlockSpec((K, D), lambda i: (1, 0)),
    ]
    out_specs = [pl.BlockSpec((tm, D), lambda i: (i, 0))]
    out_shape = [jax.ShapeDtypeStruct((M, D), F32)]
    operands = [x2, oa2, ob2, mods_l, gpost, w_out, w_out]
    n_next = _add_cast_ahead(next_w, M // tm, lambda i: i, in_specs, out_specs, out_shape, operands)
    outs = pl.pallas_call(
        functools.partial(_outproj_kernel, n_next=n_next),
        grid=(M // tm,),
        in_specs=in_specs,
        out_specs=out_specs,
        out_shape=out_shape,
        compiler_params=_cparams(("arbitrary",)),
    )(*operands)
    return outs[0], tuple(outs[1:])


def kernel(x, c, ada_w, ada_b, norm_pre, norm_post, ffn1_w_gate, ffn1_w_up, ffn1_w_down,
           mix_w_in, mix_out_gain, mix_w_out, rel_bias, ffn2_w_gate, ffn2_w_up, ffn2_w_down):
    B, S, D = x.shape
    L = ada_w.shape[0]
    n_heads = rel_bias.shape[1]
    d_mix = mix_w_out.shape[1]
    d_grp = n_heads * HEAD_DIM
    assert S % MOBA_BLOCK == 0 and d_mix == 2 * d_grp and mix_w_in.shape[-1] == 3 * d_mix

    ffn1 = [ffn1_w_gate, ffn1_w_up, ffn1_w_down]
    ffn2 = [ffn2_w_gate, ffn2_w_up, ffn2_w_down]
    assert B <= SUBLANES
    c_pad = jnp.pad(c, ((0, SUBLANES - B), (0, 0)))
    mods, (*w1, w_in) = _ada_mods(c_pad, ada_w, ada_b, (0, ffn1 + [mix_w_in]))
    mods = mods[:, :B].reshape(L, B, N_SUB * N_MOD, D)
    rel_bias_t = rel_bias.T
    tables = _bias_tables(rel_bias_t)

    x2 = x.reshape(B * S, D)
    for l in range(L):
        gpre = norm_pre[l].reshape(N_SUB, 1, D)
        gpost = norm_post[l].reshape(N_SUB, 1, D)
        gain2 = mix_out_gain[l].reshape(1, d_mix)
        x2, w2 = _ffn_sublayer(x2, mods[l], gpre[0], gpost[0], w1, (l, ffn2), sub=0, seq=S)
        proj, (w_out,) = _inproj(x2, mods[l], gpre[1], w_in, (l, [mix_w_out]), batch=B, seq=S)
        o_a, o_b = _attention(proj, rel_bias_t, tables, gain2, n_heads=n_heads)
        x2, w_in = _outproj(x2, o_a.reshape(B * S, d_grp), o_b.reshape(B * S, d_grp), mods[l], gpost[1], w_out,
                            (l + 1, [mix_w_in]) if l + 1 < L else None, seq=S)
        w_in = w_in[0] if w_in else None
        x2, w1 = _ffn_sublayer(x2, mods[l], gpre[2], gpost[2], w2, (l + 1, ffn1) if l + 1 < L else None,
                               sub=2, seq=S)
    return x2.reshape(B, S, D)
```

```python
import functools
import math

import numpy as np
import jax
import jax.numpy as jnp
from jax import lax
from jax.experimental import pallas as pl
from jax.experimental.pallas import tpu as pltpu

F32 = jnp.float32
BF16 = jnp.bfloat16

HEAD_DIM = 128
MOBA_BLOCK = 256
MOBA_TOPK = 3
REL_BUCKETS = 32
REL_MAX_DIST = 128
N_SUB = 3
N_MOD = 3
EPS = 1e-6
SB_BLOCK = 256
SB_LOG_ZERO = -104.0
SB_NEAR_BLOCKS = 3
LOG2E = math.log2(math.e)
FFN_DOWN_COLS = 512
QK_SCALE = HEAD_DIM ** -0.5 * LOG2E
SUBLANES = 8
ROW_CHUNK = 2 * SUBLANES
ROW_UNROLL = 8

V7X_VMEM_BYTES = 64 * 1024 * 1024
VMEM_LIMIT = V7X_VMEM_BYTES - 2 * 1024 * 1024


def _cparams(sem):
    return pltpu.CompilerParams(dimension_semantics=sem, vmem_limit_bytes=VMEM_LIMIT)


def _dot(a, b):
    return jnp.dot(a, b, preferred_element_type=F32)


def _dot_nt(a, b):
    return lax.dot_general(a, b, (((1,), (1,)), ((), ())), preferred_element_type=F32)


def _rms(x):
    return x * lax.rsqrt(jnp.mean(x * x, axis=-1, keepdims=True) + EPS)


def _row_chunks(n_rows, body):
    def step(r, carry):
        body(pl.ds(pl.multiple_of(r * ROW_CHUNK, ROW_CHUNK), ROW_CHUNK))
        return carry

    lax.fori_loop(0, n_rows // ROW_CHUNK, step, 0, unroll=ROW_UNROLL)


def _modulated_prenorm(x_ref, gpre_ref, mod_ref, sub, h_ref, zero_ref=None):
    gain = gpre_ref[...] * (1.0 + mod_ref[3 * sub + 1:3 * sub + 2, :])
    shift = mod_ref[3 * sub:3 * sub + 1, :]

    def body(rows):
        h_ref[rows, :] = (_rms(x_ref[rows, :]) * gain + shift).astype(BF16)
        if zero_ref is not None:
            zero_ref[rows, :] = jnp.zeros((ROW_CHUNK, zero_ref.shape[1]), zero_ref.dtype)

    _row_chunks(x_ref.shape[0], body)


def _gated_postnorm_residual(x_ref, y_ref, gpost_ref, mod_ref, sub, res_w, o_ref):
    coef = (res_w * mod_ref[3 * sub + 2:3 * sub + 3, :]) * gpost_ref[...]

    for r in range(0, x_ref.shape[0], SUBLANES):
        rows = slice(r, r + SUBLANES)
        o_ref[rows, :] = x_ref[rows, :] + _rms(y_ref[rows, :]) * coef


def _ada_kernel(c_ref, w_ref, b_ref, *rest, n_next):
    next_f32, o_ref, next_bf16 = rest[:n_next], rest[n_next], rest[n_next + 1:]
    c = c_ref[...]
    ca = (c * jax.nn.sigmoid(c)).astype(BF16)
    o_ref[...] = _dot(ca, w_ref[...].astype(BF16)) + b_ref[...]
    _cast_blocks(next_f32, next_bf16)


def _ada_mods(c_pad, ada_w, ada_b, next_w):
    L, D, N = ada_w.shape
    tn = min(1024, D)
    n_n = N // tn
    in_specs = [
        pl.BlockSpec((SUBLANES, D), lambda l, n: (0, 0)),
        pl.BlockSpec((None, D, tn), lambda l, n: (l, 0, n)),
        pl.BlockSpec((None, 1, tn), lambda l, n: (l, 0, n)),
    ]
    out_specs = [pl.BlockSpec((None, SUBLANES, tn), lambda l, n: (l, 0, n))]
    out_shape = [jax.ShapeDtypeStruct((L, SUBLANES, N), F32)]
    operands = [c_pad, ada_w, ada_b.reshape(L, 1, N)]
    n_next = _add_cast_ahead(next_w, L * n_n, lambda l, n: l * n_n + n, in_specs, out_specs, out_shape, operands)
    outs = pl.pallas_call(
        functools.partial(_ada_kernel, n_next=n_next),
        grid=(L, n_n),
        in_specs=in_specs,
        out_specs=out_specs,
        out_shape=out_shape,
        compiler_params=_cparams(("arbitrary", "arbitrary")),
    )(*operands)
    return outs[0], tuple(outs[1:])


def _ffn_kernel(x_ref, mod_ref, gpre_ref, gpost_ref, wg_ref, wu_ref, wd_ref, *rest, sub, res_w, n_next):
    next_f32, o_ref, next_bf16, h_ref = rest[:n_next], rest[n_next], rest[n_next + 1:-1], rest[-1]
    j = pl.program_id(1)

    @pl.when(j == 0)
    def _():
        _modulated_prenorm(x_ref, gpre_ref, mod_ref, sub, h_ref, zero_ref=o_ref)

    h = h_ref[...]
    g = _dot(h, wg_ref[...])
    u = _dot(h, wu_ref[...])
    a = (g * jax.nn.sigmoid(g) * u).astype(BF16)
    n_chunks = o_ref.shape[1] // FFN_DOWN_COLS
    for c in range(n_chunks):
        cols = slice(c * FFN_DOWN_COLS, (c + 1) * FFN_DOWN_COLS)
        o_ref[:, cols] += _dot(a, wd_ref[:, cols])
        if c == 0:
            _cast_blocks(next_f32, next_bf16)

    @pl.when(j == pl.num_programs(1) - 1)
    def _():
        _gated_postnorm_residual(x_ref, o_ref, gpost_ref, mod_ref, sub, res_w, o_ref)


def _cast_rows(n_rows, n_steps):
    for r in range(16, n_rows + 1, 16):
        if n_rows % r == 0 and n_rows // r <= n_steps:
            return r
    raise ValueError((n_rows, n_steps))


def _add_cast_ahead(next_w, n_steps, step_of, in_specs, out_specs, out_shape, operands):
    if next_w is None:
        return 0
    l_next, mats = next_w
    for m in mats:
        _, rows, cols = m.shape
        r = _cast_rows(rows, n_steps)
        last = rows // r - 1
        in_specs.append(pl.BlockSpec(
            (None, r, cols), lambda *g, last=last: (l_next, jnp.minimum(step_of(*g), last), 0)))
        out_specs.append(pl.BlockSpec((r, cols), lambda *g, last=last: (jnp.minimum(step_of(*g), last), 0)))
        out_shape.append(jax.ShapeDtypeStruct((rows, cols), BF16))
        operands.append(m)
    return len(mats)


def _cast_blocks(srcs, dsts):
    for src, dst in zip(srcs, dsts):
        dst[...] = src[...].astype(BF16)


def _ffn_sublayer(x2, mods_l, gpre, gpost, w, next_w, *, sub, seq, tm=1024, tf=512):
    wg, wu, wd = w
    M, D = x2.shape
    F = wg.shape[-1]
    tpb = seq // tm
    n_i, n_j = M // tm, F // tf
    in_specs = [
        pl.BlockSpec((tm, D), lambda i, j: (i, 0)),
        pl.BlockSpec((None, N_SUB * N_MOD, D), lambda i, j: (i // tpb, 0, 0)),
        pl.BlockSpec((1, D), lambda i, j: (0, 0)),
        pl.BlockSpec((1, D), lambda i, j: (0, 0)),
        pl.BlockSpec((D, tf), lambda i, j: (0, j)),
        pl.BlockSpec((D, tf), lambda i, j: (0, j)),
        pl.BlockSpec((tf, D), lambda i, j: (j, 0)),
    ]
    out_specs = [pl.BlockSpec((tm, D), lambda i, j: (i, 0))]
    out_shape = [jax.ShapeDtypeStruct((M, D), F32)]
    operands = [x2, mods_l, gpre, gpost, wg, wu, wd]
    n_next = _add_cast_ahead(next_w, n_i * n_j, lambda i, j: i * n_j + j, in_specs, out_specs, out_shape, operands)
    outs = pl.pallas_call(
        functools.partial(_ffn_kernel, sub=sub, res_w=0.5, n_next=n_next),
        grid=(n_i, n_j),
        in_specs=in_specs,
        out_specs=out_specs,
        out_shape=out_shape,
        scratch_shapes=[pltpu.VMEM((tm, D), BF16)],
        compiler_params=_cparams(("arbitrary", "arbitrary")),
    )(*operands)
    return outs[0], tuple(outs[1:])


def _inproj_kernel(x_ref, mod_ref, gpre_ref, w_ref, *rest, heads_per_tile, tiles_per_section, n_next):
    next_f32, o_ref, next_bf16, h_ref = rest[:n_next], rest[n_next], rest[n_next + 1:-1], rest[-1]
    n = pl.program_id(1)

    @pl.when(n == 0)
    def _():
        _modulated_prenorm(x_ref, gpre_ref, mod_ref, 1, h_ref)

    res = _dot(h_ref[...], w_ref[...])
    res = res * jnp.where((n // tiles_per_section) % 3 == 0, QK_SCALE, 1.0)
    _cast_blocks(next_f32, next_bf16)
    for hh in range(heads_per_tile):
        o_ref[hh] = res[:, hh * HEAD_DIM:(hh + 1) * HEAD_DIM].astype(BF16)


def _inproj(x2, mods_l, gpre, w_in, next_w, *, batch, seq, tm=1024, tn=1024):
    M, D = x2.shape
    N = w_in.shape[-1]
    section = N // 6
    tn = min(tn, section)
    assert section % tn == 0
    tpb = seq // tm
    hpt = tn // HEAD_DIM
    n_i, n_n = M // tm, N // tn
    in_specs = [
        pl.BlockSpec((tm, D), lambda i, n: (i, 0)),
        pl.BlockSpec((None, N_SUB * N_MOD, D), lambda i, n: (i // tpb, 0, 0)),
        pl.BlockSpec((1, D), lambda i, n: (0, 0)),
        pl.BlockSpec((D, tn), lambda i, n: (0, n)),
    ]
    out_specs = [pl.BlockSpec((None, hpt, tm, HEAD_DIM), lambda i, n: (i // tpb, n, i % tpb, 0))]
    out_shape = [jax.ShapeDtypeStruct((batch, N // HEAD_DIM, seq, HEAD_DIM), BF16)]
    operands = [x2, mods_l, gpre, w_in]
    n_next = _add_cast_ahead(next_w, n_i * n_n, lambda i, n: i * n_n + n, in_specs, out_specs, out_shape, operands)
    outs = pl.pallas_call(
        functools.partial(_inproj_kernel, heads_per_tile=hpt, tiles_per_section=section // tn, n_next=n_next),
        grid=(n_i, n_n),
        in_specs=in_specs,
        out_specs=out_specs,
        out_shape=out_shape,
        scratch_shapes=[pltpu.VMEM((tm, D), BF16)],
        compiler_params=_cparams(("arbitrary", "arbitrary")),
    )(*operands)
    return outs[0], tuple(outs[1:])


def _t5_bucket_np(dist):
    n = np.maximum(dist, 0)
    max_exact = REL_BUCKETS // 2
    nf = np.maximum(n, 1).astype(np.float32)
    large = max_exact + (np.log(nf / np.float32(max_exact)) / np.float32(math.log(REL_MAX_DIST / max_exact))
                         * np.float32(REL_BUCKETS - max_exact)).astype(np.int32)
    large = np.minimum(large, REL_BUCKETS - 1)
    return np.where(n < max_exact, n, large).astype(np.int32)


def _bias_table_kernel(rb_ref, bucket_ref, o_ref):
    h = pl.program_id(0)
    for t in range(2):
        bucket = bucket_ref[t]
        tab = jnp.zeros(bucket.shape, F32)
        for b in range(REL_BUCKETS):
            tab = jnp.where(bucket == b, rb_ref[h, b], tab)
        if t == 0:
            row = lax.broadcasted_iota(jnp.int32, bucket.shape, 0)
            col = lax.broadcasted_iota(jnp.int32, bucket.shape, 1)
            tab = jnp.where(row >= col, tab, -jnp.inf)
        o_ref[t] = tab * LOG2E


def _bias_tables(rel_bias_t):
    H = rel_bias_t.shape[0]
    r = np.arange(MOBA_BLOCK)
    rel = r[:, None] - r[None, :]
    buckets = jnp.asarray(np.stack([_t5_bucket_np(rel), _t5_bucket_np(rel + MOBA_BLOCK)]))
    return pl.pallas_call(
        _bias_table_kernel,
        grid=(H,),
        in_specs=[
            pl.BlockSpec(memory_space=pltpu.SMEM),
            pl.BlockSpec((2, MOBA_BLOCK, MOBA_BLOCK), lambda h: (0, 0, 0)),
        ],
        out_specs=pl.BlockSpec((None, 2, MOBA_BLOCK, MOBA_BLOCK), lambda h: (h, 0, 0, 0)),
        out_shape=jax.ShapeDtypeStruct((H, 2, MOBA_BLOCK, MOBA_BLOCK), F32),
        compiler_params=_cparams(("arbitrary",)),
    )(rel_bias_t, buckets)


def _head_out(acc, gain):
    return (_rms(acc) * gain).astype(BF16)


def _moba_stages(h, rb_ref, q_ref, k_ref, v_ref, tab_ref, gain_ref, o_ref, s_ref, p_ref, v1_ref, *, nb):
    BLK = MOBA_BLOCK
    HALF = BLK // 2
    GATE_ROWS = 16
    assert nb <= GATE_ROWS
    b_far = rb_ref[h, REL_BUCKETS - 1] * LOG2E

    v1_ref[:, :HEAD_DIM] = v_ref[...]
    v1_ref[:, HEAD_DIM:] = jnp.ones((nb * BLK, HEAD_DIM), BF16)

    kf = k_ref[...].astype(F32).reshape(nb, BLK, HEAD_DIM)
    k_mean = jnp.mean(kf, axis=1)
    k_mean = jnp.concatenate([k_mean, jnp.zeros((GATE_ROWS - nb, HEAD_DIM), F32)], axis=0).astype(BF16)
    blk_id = lax.broadcasted_iota(jnp.int32, (GATE_ROWS, BLK), 0)

    def scores(i):
        qi = q_ref[i * BLK:(i + 1) * BLK, :]
        s_buf = s_ref.at[i % 2]
        if i > 0:
            gate = jnp.where(blk_id < i, _dot_nt(k_mean, qi), -jnp.inf)
            rank = jnp.zeros((GATE_ROWS, BLK), jnp.int32)
            for jp in range(i):
                cj = gate[jp:jp + 1, :]
                beats = (cj > gate) | ((cj == gate) & (jp < blk_id))
                rank = rank + beats.astype(jnp.int32)
            sel_t = jnp.where((rank < MOBA_TOPK) & (blk_id < i), 0.0, -jnp.inf)
            selw = jnp.concatenate([sel_t, jnp.zeros((HEAD_DIM - GATE_ROWS, BLK), F32)], axis=0).T

        m_run = None
        for j in range(i + 1):
            s = _dot_nt(qi, k_ref[j * BLK:(j + 1) * BLK, :])
            if j == i:
                s = s + tab_ref[0]
            elif j == i - 1:
                s = s + tab_ref[1] + selw[:, j:j + 1]
            else:
                s = s + (selw[:, j:j + 1] + b_far)
            s_buf[:, j * BLK:(j + 1) * BLK] = s
            mj = jnp.maximum(s[:, :HALF], s[:, HALF:])
            m_run = mj if m_run is None else jnp.maximum(m_run, mj)
        return jnp.max(m_run, axis=-1, keepdims=True)

    def output(i, m):
        s_buf = s_ref.at[i % 2]
        p_buf = p_ref.at[i % 2]
        for j in range(i + 1):
            p_buf[:, j * BLK:(j + 1) * BLK] = jnp.exp2(s_buf[:, j * BLK:(j + 1) * BLK] - m).astype(BF16)
        acc = _dot(p_buf[:, :(i + 1) * BLK], v1_ref[:(i + 1) * BLK, :])
        o_ref[i * BLK:(i + 1) * BLK, :] = _head_out(acc[:, :HEAD_DIM] / acc[:, HEAD_DIM:HEAD_DIM + 1],
                                                    gain_ref[...])

    return scores, output


def _sb_stages(q_ref, k_ref, v_ref, gain_ref, o_ref, acc_ref, rem_ref, *, nb):
    BLK = SB_BLOCK
    row = lax.broadcasted_iota(jnp.int32, (BLK, BLK), 0)
    col = lax.broadcasted_iota(jnp.int32, (BLK, BLK), 1)
    strict = col < row
    after = jnp.where(row > col, 1.0, 0.0).astype(BF16)

    def tail_sums(x):
        return _dot(x.astype(BF16), after)

    def strip_logits(qi, k_strip, diagonal):
        n = k_strip.shape[0] // BLK
        parts = []
        for b in range(n):
            z = _dot_nt(qi, k_strip[b * BLK:(b + 1) * BLK, :])
            t = jnp.exp2(z)
            e = jnp.minimum(t, 1.0 / t)
            f_b = jnp.maximum(z, 0.0) + jnp.log2(1.0 + e)
            f_m = jnp.where(strict, f_b, 0.0) if diagonal and b == n - 1 else f_b
            w = tail_sums(f_m)
            parts.append((z - f_b - w, w[:, 0:1] + f_m[:, 0:1]))
        return parts

    def strip_output(parts, v_strip, rem, diagonal):
        n = len(parts)
        weights = [None] * n
        for b in reversed(range(n)):
            log_a, spent = parts[b]
            a = jnp.exp2(log_a if rem is None else log_a + rem)
            if diagonal and b == n - 1:
                a = jnp.where(strict, a, 0.0)
            weights[b] = a.astype(BF16)
            rem = -spent if rem is None else rem - spent
        a_all = weights[0] if n == 1 else jnp.concatenate(weights, axis=1)
        return _dot(a_all, v_strip), rem

    def strip(qi, k_strip, v_strip, rem, diagonal):
        return strip_output(strip_logits(qi, k_strip, diagonal), v_strip, rem, diagonal)

    def alive(rem):
        return (jnp.max(rem) > SB_LOG_ZERO * LOG2E).astype(jnp.int32)

    def near_keys(i):
        return slice(max(0, i - (SB_NEAR_BLOCKS - 1)) * BLK, (i + 1) * BLK)

    pending = []

    def logits(i):
        return strip_logits(q_ref[i * BLK:(i + 1) * BLK, :], k_ref[near_keys(i), :], True)

    def output(i, parts):
        acc, rem = strip_output(parts, v_ref[near_keys(i), :], None, True)
        if near_keys(i).start == 0:
            o_ref[i * BLK:(i + 1) * BLK, :] = _head_out(acc, gain_ref[...])
        else:
            acc_ref[i] = acc
            rem_ref[i] = jnp.broadcast_to(rem, (BLK, HEAD_DIM))
            pending.append((i, near_keys(i).start // BLK - 1, alive(rem)))

    def older():
        for i, next_blk, alive0 in pending:
            qi = q_ref[i * BLK:(i + 1) * BLK, :]

            def trip(c, n, qi=qi):
                j, _, rem, acc = c
                keys = pl.ds(pl.multiple_of((j - (n - 1)) * BLK, BLK), n * BLK)
                add, rem = strip(qi, k_ref[keys, :], v_ref[keys, :], rem, False)
                return j - n, alive(rem), rem, acc + add

            state = (jnp.int32(next_blk), alive0, rem_ref[i][:, 0:1], acc_ref[i])
            if next_blk >= 1:
                state = lax.while_loop(lambda c: (c[0] >= 1) & (c[1] > 0), functools.partial(trip, n=2), state)
            state = lax.while_loop(lambda c: (c[0] >= 0) & (c[1] > 0), functools.partial(trip, n=1), state)
            o_ref[i * BLK:(i + 1) * BLK, :] = _head_out(state[3], gain_ref[...])

    return logits, output, older


def _attn_kernel(rb_ref, qa_ref, ka_ref, va_ref, qb_ref, kb_ref, vb_ref, tab_ref, gain_a_ref, gain_b_ref,
                 oa_ref, ob_ref, s_ref, p_ref, v1_ref, acc_ref, rem_ref, *, nb):
    g = pl.program_id(1)
    a_scores, a_output = _moba_stages(g, rb_ref, qa_ref, ka_ref, va_ref, tab_ref, gain_a_ref, oa_ref,
                                      s_ref, p_ref, v1_ref, nb=nb)
    b_logits, b_output, b_older = _sb_stages(qb_ref, kb_ref, vb_ref, gain_b_ref, ob_ref, acc_ref, rem_ref, nb=nb)
    m, parts = a_scores(0), b_logits(0)
    for i in range(nb):
        if i + 1 < nb:
            m_next, parts_next = a_scores(i + 1), b_logits(i + 1)
        a_output(i, m)
        b_output(i, parts)
        m, parts = m_next, parts_next
    b_older()


def _attention(proj, rel_bias_t, tables, gain2, *, n_heads):
    B, _, S, _ = proj.shape
    nb = S // MOBA_BLOCK
    blk = (None, None, S, HEAD_DIM)

    def col_head(which):
        return pl.BlockSpec(blk, lambda b, g: (b, which * n_heads + g, 0, 0))

    out_spec = pl.BlockSpec((None, S, HEAD_DIM), lambda b, g: (b, 0, g))
    out_shape = jax.ShapeDtypeStruct((B, S, n_heads * HEAD_DIM), BF16)
    return pl.pallas_call(
        functools.partial(_attn_kernel, nb=nb),
        grid=(B, n_heads),
        in_specs=[
            pl.BlockSpec(memory_space=pltpu.SMEM),
            *[col_head(which) for which in range(6)],
            pl.BlockSpec((None, 2, MOBA_BLOCK, MOBA_BLOCK), lambda b, g: (g, 0, 0, 0)),
            pl.BlockSpec((1, HEAD_DIM), lambda b, g: (0, g)),
            pl.BlockSpec((1, HEAD_DIM), lambda b, g: (0, n_heads + g)),
        ],
        out_specs=[out_spec, out_spec],
        out_shape=[out_shape, out_shape],
        scratch_shapes=[
            pltpu.VMEM((2, MOBA_BLOCK, S), F32),
            pltpu.VMEM((2, MOBA_BLOCK, S), BF16),
            pltpu.VMEM((S, 2 * HEAD_DIM), BF16),
            pltpu.VMEM((nb, SB_BLOCK, HEAD_DIM), F32),
            pltpu.VMEM((nb, SB_BLOCK, HEAD_DIM), F32),
        ],
        compiler_params=_cparams(("parallel", "parallel")),
    )(rel_bias_t, *([proj] * 6), tables, gain2, gain2)


def _outproj_kernel(x_ref, oa_ref, ob_ref, mod_ref, gpost_ref, wa_ref, wb_ref, *rest, n_next):
    next_f32, out_ref, next_bf16 = rest[:n_next], rest[n_next], rest[n_next + 1:]
    out_ref[...] = _dot(oa_ref[...], wa_ref[...]) + _dot(ob_ref[...], wb_ref[...])
    _cast_blocks(next_f32, next_bf16)
    _gated_postnorm_residual(x_ref, out_ref, gpost_ref, mod_ref, 1, 1.0, out_ref)


def _outproj(x2, oa2, ob2, mods_l, gpost, w_out, next_w, *, seq, tm=512):
    M, D = x2.shape
    K = oa2.shape[-1]
    tpb = seq // tm
    in_specs = [
        pl.BlockSpec((tm, D), lambda i: (i, 0)),
        pl.BlockSpec((tm, K), lambda i: (i, 0)),
        pl.BlockSpec((tm, K), lambda i: (i, 0)),
        pl.BlockSpec((None, N_SUB * N_MOD, D), lambda i: (i // tpb, 0, 0)),
        pl.BlockSpec((1, D), lambda i: (0, 0)),
        pl.BlockSpec((K, D), lambda i: (0, 0)),
        pl.BlockSpec((K, D), lambda i: (1, 0)),
    ]
    out_specs = [pl.BlockSpec((tm, D), lambda i: (i, 0))]
    out_shape = [jax.ShapeDtypeStruct((M, D), F32)]
    operands = [x2, oa2, ob2, mods_l, gpost, w_out, w_out]
    n_next = _add_cast_ahead(next_w, M // tm, lambda i: i, in_specs, out_specs, out_shape, operands)
    outs = pl.pallas_call(
        functools.partial(_outproj_kernel, n_next=n_next),
        grid=(M // tm,),
        in_specs=in_specs,
        out_specs=out_specs,
        out_shape=out_shape,
        compiler_params=_cparams(("arbitrary",)),
    )(*operands)
    return outs[0], tuple(outs[1:])


def kernel(x, c, ada_w, ada_b, norm_pre, norm_post, ffn1_w_gate, ffn1_w_up, ffn1_w_down,
           mix_w_in, mix_out_gain, mix_w_out, rel_bias, ffn2_w_gate, ffn2_w_up, ffn2_w_down):
    B, S, D = x.shape
    L = ada_w.shape[0]
    n_heads = rel_bias.shape[1]
    d_mix = mix_w_out.shape[1]
    d_grp = n_heads * HEAD_DIM
    assert S % MOBA_BLOCK == 0 and d_mix == 2 * d_grp and mix_w_in.shape[-1] == 3 * d_mix

    ffn1 = [ffn1_w_gate, ffn1_w_up, ffn1_w_down]
    ffn2 = [ffn2_w_gate, ffn2_w_up, ffn2_w_down]
    assert B <= SUBLANES
    c_pad = jnp.pad(c, ((0, SUBLANES - B), (0, 0)))
    mods, (*w1, w_in) = _ada_mods(c_pad, ada_w, ada_b, (0, ffn1 + [mix_w_in]))
    mods = mods[:, :B].reshape(L, B, N_SUB * N_MOD, D)
    rel_bias_t = rel_bias.T
    tables = _bias_tables(rel_bias_t)

    x2 = x.reshape(B * S, D)
    for l in range(L):
        gpre = norm_pre[l].reshape(N_SUB, 1, D)
        gpost = norm_post[l].reshape(N_SUB, 1, D)
        gain2 = mix_out_gain[l].reshape(1, d_mix)
        x2, w2 = _ffn_sublayer(x2, mods[l], gpre[0], gpost[0], w1, (l, ffn2), sub=0, seq=S)
        proj, (w_out,) = _inproj(x2, mods[l], gpre[1], w_in, (l, [mix_w_out]), batch=B, seq=S)
        o_a, o_b = _attention(proj, rel_bias_t, tables, gain2, n_heads=n_heads)
        x2, w_in = _outproj(x2, o_a.reshape(B * S, d_grp), o_b.reshape(B * S, d_grp), mods[l], gpost[1], w_out,
                            (l + 1, [mix_w_in]) if l + 1 < L else None, seq=S)
        w_in = w_in[0] if w_in else None
        x2, w1 = _ffn_sublayer(x2, mods[l], gpre[2], gpost[2], w2, (l + 1, ffn1) if l + 1 < L else None,
                               sub=2, seq=S)
    return x2.reshape(B, S, D)
```
